```python
import jax, jax.numpy as jnp
from jax import lax
import numpy as np

D_MODEL = 1024
BATCH = 4
SEQ = 4096
DEPTH = 2
DEC_BATCH = 32
DEC_SEQ = 1
PAST_LEN = 8192
PAGE_SIZE = 128

SB_HEADS = 8
SB_HEAD_DIM = 64
D_SB = SB_HEADS * SB_HEAD_DIM
Q_BLOCK = 128
SB_BIAS_INIT = -7.0
POOL_WINDOWS = (2, 4, 8, 16)
N_POOL_GROUPS = len(POOL_WINDOWS)
D_POOL = D_MODEL // 2
POOL_GD = D_POOL // N_POOL_GROUPS
POOL_HIST = max(POOL_WINDOWS) - 1
D_CONV = D_MODEL
CONV_WIDTH = 31
CONV_HIST = CONV_WIDTH - 1
N_EXPERT_GROUPS = 4
EXPERTS_PER_GROUP = 4
N_EXPERTS = N_EXPERT_GROUPS * EXPERTS_PER_GROUP
TOP_K_IN_GROUP = 2
D_EXPERT = D_MODEL // 2
D_PLE = 256
N_AB_LAYERS = (DEPTH + 1) // 2
N_C_LAYERS = DEPTH // 2
EPS = 1e-6

kernel_name = 'stickbreak_pool_conformer_hmoe_decoder_step'


def rmsnorm(x, g):
    xf = x.astype(jnp.float32)
    y = xf * lax.rsqrt(jnp.mean(xf * xf, axis=-1, keepdims=True) + EPS)
    return (y * g.astype(jnp.float32)).astype(x.dtype)


def layernorm(x, g, b):
    xf = x.astype(jnp.float32)
    mu = jnp.mean(xf, axis=-1, keepdims=True)
    xc = xf - mu
    y = xc * lax.rsqrt(jnp.mean(xc * xc, axis=-1, keepdims=True) + EPS)
    return (y * g.astype(jnp.float32) + b.astype(jnp.float32)).astype(x.dtype)


def sb_block(q, k, v, bias, q_pos, k_pos):
    z = jnp.einsum('bqhd,bkhd->bhqk', q, k, preferred_element_type=jnp.float32) * (SB_HEAD_DIM ** -0.5)
    z = z + bias.astype(jnp.float32)[None, :, None, None]
    causal = (k_pos[None, :] < q_pos[:, None])[None, None]
    log_stay = jnp.where(causal, jax.nn.log_sigmoid(-z), 0.0)
    between = lax.cumsum(log_stay, axis=3, reverse=True) - log_stay
    weight = jnp.where(causal, jnp.exp(jax.nn.log_sigmoid(z) + between), 0.0)
    return jnp.einsum('bhqk,bkhd->bqhd', weight.astype(v.dtype), v)


def sb_attention(q, k, v, bias, pos0):
    b, tq, h, dh = q.shape
    tk = k.shape[1]
    k_pos = jnp.arange(tk)
    q_pos = pos0 + jnp.arange(tq)
    if tq <= Q_BLOCK:
        return sb_block(q, k, v, bias, q_pos, k_pos)
    nb = tq // Q_BLOCK
    qb = q.reshape(b, nb, Q_BLOCK, h, dh).transpose(1, 0, 2, 3, 4)
    pb = q_pos.reshape(nb, Q_BLOCK)
    ob = lax.map(lambda a: sb_block(a[0], k, v, bias, a[1], k_pos), (qb, pb))
    return ob.transpose(1, 0, 2, 3, 4).reshape(b, tq, h, dh)


def multiscale_pool(u, hist, pos0):
    t = u.shape[1]
    xp = jnp.concatenate([hist.astype(u.dtype), u], axis=1)
    xf = xp.astype(jnp.float32)
    cs = jnp.concatenate([jnp.zeros_like(xf[:, :1]), jnp.cumsum(xf, axis=1)], axis=1)
    pos = pos0 + jnp.arange(t)
    outs = []
    for g, w in enumerate(POOL_WINDOWS):
        c = slice(g * POOL_GD, (g + 1) * POOL_GD)
        s = cs[:, POOL_HIST + 1:POOL_HIST + 1 + t, c] - cs[:, POOL_HIST + 1 - w:POOL_HIST + 1 - w + t, c]
        cnt = jnp.minimum(pos + 1, w).astype(jnp.float32)
        outs.append(s / cnt[None, :, None])
    pooled = jnp.concatenate(outs, axis=-1) - u.astype(jnp.float32)
    return pooled.astype(u.dtype), xp[:, -POOL_HIST:]


def conformer_conv(h, hist, w_in, w_dw, b_dw, ln_g, ln_b, w_out):
    a, g = jnp.split(h @ w_in, 2, axis=-1)
    u = a * jax.nn.sigmoid(g)
    up = jnp.concatenate([hist.astype(u.dtype), u], axis=1)
    y = lax.conv_general_dilated(up, w_dw[:, None, :].astype(up.dtype), window_strides=(1,), padding='VALID',
                                 dimension_numbers=('NWC', 'WIO', 'NWC'), feature_group_count=D_CONV) + b_dw
    y = jax.nn.silu(layernorm(y, ln_g, ln_b))
    return y @ w_out, up[:, -CONV_HIST:]


def hier_moe(x, w_rg, b_rg, w_re, b_re, w_gate, w_up, w_down):
    shp = x.shape
    xf = x.reshape(-1, D_MODEL)
    n = xf.shape[0]
    g_logits = (xf @ w_rg + b_rg).astype(jnp.float32)
    g_prob = jax.nn.softmax(g_logits, axis=-1)
    _, g_idx = lax.top_k(g_logits, 1)
    p_group = jnp.take_along_axis(g_prob, g_idx, axis=-1)
    e_logits = (xf @ w_re + b_re).astype(jnp.float32).reshape(n, N_EXPERT_GROUPS, EXPERTS_PER_GROUP)
    e_in = jnp.take_along_axis(e_logits, g_idx[:, :, None], axis=1)[:, 0]
    e_val, e_idx = lax.top_k(e_in, TOP_K_IN_GROUP)
    w_sel = jax.nn.softmax(e_val, axis=-1) * p_group
    expert_id = g_idx * EXPERTS_PER_GROUP + e_idx
    combine = jnp.sum(jax.nn.one_hot(expert_id, N_EXPERTS, dtype=jnp.float32) * w_sel[..., None], axis=1).astype(x.dtype)
    y = jnp.zeros_like(xf)
    for e in range(N_EXPERTS):
        hid = jax.nn.silu(xf @ w_gate[e]) * (xf @ w_up[e])
        y = y + combine[:, e:e + 1] * (hid @ w_down[e])
    return y.reshape(shp)


def per_layer_embedding(h, p, w_ple, w_pg, b_pg, g_pg):
    gate = jax.nn.sigmoid((rmsnorm(h, g_pg) @ w_pg + b_pg).astype(jnp.float32))
    return h + (gate * (p @ w_ple).astype(jnp.float32)).astype(h.dtype)


def trunk(x, p, pos0, past_k, past_v, pool_hist, conv_hist, prm):
    b, t, _ = x.shape
    h = x
    new_k, new_v, new_pool, new_conv = [], [], [], []
    for i in range(DEPTH):
        hn = rmsnorm(h, prm['norm_mix'][i])
        li = i // 2
        if i % 2 == 0:
            proj = hn @ prm['w_in_ab'][li]
            q = proj[..., :D_SB].reshape(b, t, SB_HEADS, SB_HEAD_DIM)
            k = proj[..., D_SB:2 * D_SB].reshape(b, t, SB_HEADS, SB_HEAD_DIM)
            v = proj[..., 2 * D_SB:3 * D_SB].reshape(b, t, SB_HEADS, SB_HEAD_DIM)
            u = proj[..., 3 * D_SB:]
            k_all = jnp.concatenate([past_k[li].astype(k.dtype), k], axis=1)
            v_all = jnp.concatenate([past_v[li].astype(v.dtype), v], axis=1)
            o_sb = sb_attention(q, k_all, v_all, prm['sb_bias'][li], pos0).reshape(b, t, D_SB)
            pooled, pool_state = multiscale_pool(u, pool_hist[li], pos0)
            o_pool = jnp.einsum('btgc,gcd->btgd', pooled.reshape(b, t, N_POOL_GROUPS, POOL_GD),
                                prm['w_pool'][li]).reshape(b, t, D_POOL) * prm['pool_scale'][li]
            mix = jnp.concatenate([o_sb, o_pool], axis=-1) @ prm['w_out_ab'][li]
            new_k.append(k)
            new_v.append(v)
            new_pool.append(pool_state)
        else:
            mix, conv_state = conformer_conv(hn, conv_hist[li], prm['w_in_c'][li], prm['w_dw'][li], prm['b_dw'][li],
                                             prm['ln_g_c'][li], prm['ln_b_c'][li], prm['w_out_c'][li])
            new_conv.append(conv_state)
        h = h + mix
        h = h + hier_moe(rmsnorm(h, prm['norm_ffn'][i]), prm['w_rg'][i], prm['b_rg'][i], prm['w_re'][i], prm['b_re'][i],
                         prm['w_gate_e'][i], prm['w_up_e'][i], prm['w_down_e'][i])
        h = per_layer_embedding(h, p[i], prm['w_ple'][i], prm['w_ple_gate'][i], prm['b_ple_gate'][i], prm['g_ple_gate'][i])
    y = rmsnorm(h, prm['norm_final'])
    return y, jnp.stack(new_k), jnp.stack(new_v), jnp.stack(new_pool), jnp.stack(new_conv)


def setup_inputs(seed: int = 0) -> dict:
    key = jax.random.key(seed)
    ks = iter(jax.random.split(key, 48))

    def nrm(shape, scale=1.0):
        return jax.random.normal(next(ks), shape, jnp.float32) * scale

    n_pages = PAST_LEN // PAGE_SIZE
    n_used = DEC_BATCH * n_pages
    n_phys = n_used + n_used // 4
    page_table = jax.random.permutation(next(ks), n_phys)[:n_used].reshape(DEC_BATCH, n_pages).astype(jnp.int32)
    return {
        'x_prompt': nrm((BATCH, SEQ, D_MODEL)),
        'x_sample': nrm((DEC_BATCH, DEC_SEQ, D_MODEL)),
        'p_prompt': nrm((DEPTH, BATCH, SEQ, D_PLE)),
        'p_sample': nrm((DEPTH, DEC_BATCH, DEC_SEQ, D_PLE)),
        'cache_k': nrm((N_AB_LAYERS, n_phys, PAGE_SIZE, SB_HEADS, SB_HEAD_DIM)),
        'cache_v': nrm((N_AB_LAYERS, n_phys, PAGE_SIZE, SB_HEADS, SB_HEAD_DIM)),
        'state_pool': nrm((N_AB_LAYERS, DEC_BATCH, POOL_HIST, D_POOL)),
        'state_conv': nrm((N_C_LAYERS, DEC_BATCH, CONV_HIST, D_CONV), 0.5),
        'page_table': page_table,
        'norm_mix': 1.0 + nrm((DEPTH, D_MODEL), 0.02),
        'norm_ffn': 1.0 + nrm((DEPTH, D_MODEL), 0.02),
        'norm_final': 1.0 + nrm((D_MODEL,), 0.02),
        'w_in_ab': nrm((N_AB_LAYERS, D_MODEL, 3 * D_SB + D_POOL), D_MODEL ** -0.5),
        'sb_bias': SB_BIAS_INIT + nrm((N_AB_LAYERS, SB_HEADS), 0.5),
        'w_pool': nrm((N_AB_LAYERS, N_POOL_GROUPS, POOL_GD, POOL_GD), POOL_GD ** -0.5),
        'pool_scale': 1.0 + nrm((N_AB_LAYERS, D_POOL), 0.02),
        'w_out_ab': nrm((N_AB_LAYERS, D_SB + D_POOL, D_MODEL), (D_SB + D_POOL) ** -0.5),
        'w_in_c': nrm((N_C_LAYERS, D_MODEL, 2 * D_CONV), D_MODEL ** -0.5),
        'w_dw': nrm((N_C_LAYERS, CONV_WIDTH, D_CONV), CONV_WIDTH ** -0.5),
        'b_dw': nrm((N_C_LAYERS, D_CONV), 0.02),
        'ln_g_c': 1.0 + nrm((N_C_LAYERS, D_CONV), 0.02),
        'ln_b_c': nrm((N_C_LAYERS, D_CONV), 0.02),
        'w_out_c': nrm((N_C_LAYERS, D_CONV, D_MODEL), D_CONV ** -0.5),
        'w_rg': nrm((DEPTH, D_MODEL, N_EXPERT_GROUPS), D_MODEL ** -0.5),
        'b_rg': nrm((DEPTH, N_EXPERT_GROUPS), 0.01),
        'w_re': nrm((DEPTH, D_MODEL, N_EXPERTS), D_MODEL ** -0.5),
        'b_re': nrm((DEPTH, N_EXPERTS), 0.01),
        'w_gate_e': nrm((DEPTH, N_EXPERTS, D_MODEL, D_EXPERT), D_MODEL ** -0.5),
        'w_up_e': nrm((DEPTH, N_EXPERTS, D_MODEL, D_EXPERT), D_MODEL ** -0.5),
        'w_down_e': nrm((DEPTH, N_EXPERTS, D_EXPERT, D_MODEL), D_EXPERT ** -0.5),
        'w_ple': nrm((DEPTH, D_PLE, D_MODEL), D_PLE ** -0.5),
        'w_ple_gate': nrm((DEPTH, D_MODEL, D_MODEL), D_MODEL ** -0.5),
        'b_ple_gate': nrm((DEPTH, D_MODEL), 0.02),
        'g_ple_gate': 1.0 + nrm((DEPTH, D_MODEL), 0.02),
    }


def reference(x_prompt, x_sample, p_prompt, p_sample, cache_k, cache_v, state_pool, state_conv, page_table,
              norm_mix, norm_ffn, norm_final, w_in_ab, sb_bias, w_pool, pool_scale, w_out_ab,
              w_in_c, w_dw, b_dw, ln_g_c, ln_b_c, w_out_c,
              w_rg, b_rg, w_re, b_re, w_gate_e, w_up_e, w_down_e,
              w_ple, w_ple_gate, b_ple_gate, g_ple_gate):
    prm = dict(norm_mix=norm_mix, norm_ffn=norm_ffn, norm_final=norm_final, w_in_ab=w_in_ab, sb_bias=sb_bias,
               w_pool=w_pool, pool_scale=pool_scale, w_out_ab=w_out_ab, w_in_c=w_in_c, w_dw=w_dw, b_dw=b_dw,
               ln_g_c=ln_g_c, ln_b_c=ln_b_c, w_out_c=w_out_c, w_rg=w_rg, b_rg=b_rg, w_re=w_re, b_re=b_re,
               w_gate_e=w_gate_e, w_up_e=w_up_e, w_down_e=w_down_e, w_ple=w_ple, w_ple_gate=w_ple_gate,
               b_ple_gate=b_ple_gate, g_ple_gate=g_ple_gate)
    bp = x_prompt.shape[0]
    n_ab = w_in_ab.shape[0]
    n_c = w_in_c.shape[0]
    empty_kv = jnp.zeros((n_ab, bp, 0, SB_HEADS, SB_HEAD_DIM), x_prompt.dtype)
    zero_pool = jnp.zeros((n_ab, bp, POOL_HIST, D_POOL), x_prompt.dtype)
    zero_conv = jnp.zeros((n_c, bp, CONV_HIST, D_CONV), x_prompt.dtype)
    y_prompt, k_prompt, v_prompt, pool_prompt, conv_prompt = trunk(
        x_prompt, p_prompt, 0, empty_kv, empty_kv, zero_pool, zero_conv, prm)
    db, n_pages = page_table.shape
    past_len = n_pages * PAGE_SIZE
    past_k = cache_k[:, page_table].reshape(n_ab, db, past_len, SB_HEADS, SB_HEAD_DIM)
    past_v = cache_v[:, page_table].reshape(n_ab, db, past_len, SB_HEADS, SB_HEAD_DIM)
    y_sample, k_sample, v_sample, pool_sample, conv_sample = trunk(
        x_sample, p_sample, past_len, past_k, past_v, state_pool, state_conv, prm)
    return (y_prompt, y_sample, k_prompt, v_prompt, pool_prompt, conv_prompt, k_sample, v_sample, pool_sample, conv_sample)
```

```python
import functools

import jax
import jax.numpy as jnp
from jax import lax
from jax.experimental import pallas as pl
from jax.experimental.pallas import tpu as pltpu

F32 = jnp.float32
BF16 = jnp.bfloat16
I32 = jnp.int32

EPS = 1e-6
SB_HEADS = 8
SB_HEAD_DIM = 64
POOL_WINDOWS = (2, 4, 8, 16)
POOL_HIST = max(POOL_WINDOWS) - 1
CONV_WIDTH = 31
CONV_HIST = CONV_WIDTH - 1
N_EXPERT_GROUPS = 4
EXPERTS_PER_GROUP = 4
N_EXPERTS = N_EXPERT_GROUPS * EXPERTS_PER_GROUP
PAIRS = ((0, 1), (0, 2), (0, 3), (1, 3), (1, 2), (2, 3))
N_BUCKETS = N_EXPERT_GROUPS * len(PAIRS)

LANES = 128
TM = 512
SP = 128
TQ = 256
TMX = 256
SORT_CHUNK = 512
NB_ROWS = 32
ROUTE_ROWS = 32
HALO_POOL = 16
HALO_CONV = 32
PAGES_PER_STEP = 16
DMA_RING = 16
NEG_BIG = -1e30
VMEM_LIMIT = 56 * 1024 * 1024


def _cparams(n_axes=1, vmem=VMEM_LIMIT):
    return pltpu.CompilerParams(dimension_semantics=("arbitrary",) * n_axes, vmem_limit_bytes=vmem)


def _rms(x, g):
    return x * lax.rsqrt(jnp.mean(x * x, axis=-1, keepdims=True) + EPS) * g


def _sigmoid(x):
    return 1.0 / (1.0 + jnp.exp(-x))


def _softplus(z):
    return jnp.maximum(z, 0.0) + jnp.log(1.0 + jnp.exp(-jnp.abs(z)))


def _dot(a, b):
    return jnp.dot(a, b, preferred_element_type=F32)


def _dot_nt(a, b):
    return lax.dot_general(a, b, (((1,), (1,)), ((), ())), preferred_element_type=F32)


def _split(a):
    hi = a.astype(BF16)
    return hi, (a - hi.astype(F32)).astype(BF16)


def _dot3(a_hi, a_lo, w_hi, w_lo):
    return _dot(a_hi, w_hi) + (_dot(a_hi, w_lo) + _dot(a_lo, w_hi))


def _mm(a, w_ref, wlo_ref, precise, rows=slice(None), cols=slice(None)):
    if precise:
        return _dot3(*_split(a), w_ref[rows, cols], wlo_ref[rows, cols])
    return _dot(a.astype(BF16), w_ref[rows, cols])


def _hi_lo(w):
    hi = w.astype(BF16)
    return hi, (w - hi.astype(F32)).astype(BF16)


def _pspec(cols, npt, rows=TM):
    return pl.BlockSpec((rows, cols), lambda i: (jnp.minimum(i, npt - 1), 0))


def _const_spec(shape):
    nd = len(shape)
    return pl.BlockSpec(shape, lambda i: (0,) * nd)


def _inproj_ab_kernel(xp_ref, xs_ref, g_ref, w_ref, wlo_ref, wt_ref,
                      qp_ref, vbp_ref, up_ref, ktp_ref, vtp_ref, kbtp_ref,
                      qs_ref, us_ref, ks_ref, vs_ref, *, npt, d_sb):
    i = pl.program_id(0)
    ucols = slice(3 * d_sb, None)

    @pl.when(i < npt)
    def _():
        hn, hn_lo = _split(_rms(xp_ref[...], g_ref[...]))
        pr = _dot(hn, w_ref[...])
        qp_ref[...] = (pr[:, :d_sb] * (SB_HEAD_DIM ** -0.5)).astype(BF16)
        vbp_ref[...] = pr[:, 2 * d_sb:3 * d_sb].astype(BF16)
        up_ref[...] = pr[:, ucols] + (_dot(hn, wlo_ref[:, ucols]) + _dot(hn_lo, w_ref[:, ucols]))
        kvt = _dot_nt(wt_ref[...], hn)
        ktp_ref[0] = kvt[:d_sb]
        vtp_ref[0] = kvt[d_sb:]
        kbtp_ref[0] = kvt[:d_sb].astype(BF16)

    @pl.when(i == npt)
    def _():
        pr = _mm(_rms(xs_ref[...], g_ref[...]), w_ref, wlo_ref, True)
        qs_ref[...] = pr[:, :d_sb] * (SB_HEAD_DIM ** -0.5)
        ks_ref[...] = pr[:, d_sb:2 * d_sb]
        vs_ref[...] = pr[:, 2 * d_sb:3 * d_sb]
        us_ref[...] = pr[:, ucols]


def _inproj_ab(x_p, x_s, g, w, w_lo, wt, *, batch, seq):
    n, d = x_p.shape
    npt = n // TM
    tps = seq // TM
    d_sb = SB_HEADS * SB_HEAD_DIM
    d_pool = w.shape[1] - 3 * d_sb
    tspec = pl.BlockSpec((1, d_sb, TM), lambda i: (jnp.minimum(i, npt - 1) // tps, 0, jnp.minimum(i, npt - 1) % tps))
    out_shape = (
        jax.ShapeDtypeStruct((n, d_sb), BF16),
        jax.ShapeDtypeStruct((n, d_sb), BF16),
        jax.ShapeDtypeStruct((n, d_pool), F32),
        jax.ShapeDtypeStruct((batch, d_sb, seq), F32),
        jax.ShapeDtypeStruct((batch, d_sb, seq), F32),
        jax.ShapeDtypeStruct((batch, d_sb, seq), BF16),
        jax.ShapeDtypeStruct((SP, d_sb), F32),
        jax.ShapeDtypeStruct((SP, d_pool), F32),
        jax.ShapeDtypeStruct((SP, d_sb), F32),
        jax.ShapeDtypeStruct((SP, d_sb), F32),
    )
    out_specs = (
        _pspec(d_sb, npt), _pspec(d_sb, npt), _pspec(d_pool, npt), tspec, tspec, tspec,
        _const_spec((SP, d_sb)), _const_spec((SP, d_pool)), _const_spec((SP, d_sb)), _const_spec((SP, d_sb)),
    )
    return pl.pallas_call(
        functools.partial(_inproj_ab_kernel, npt=npt, d_sb=d_sb),
        grid=(npt + 1,),
        in_specs=[_pspec(d, npt), _const_spec((SP, d)), _const_spec((1, d)), _const_spec(w.shape), _const_spec(w_lo.shape),
                  _const_spec(wt.shape)],
        out_specs=out_specs,
        out_shape=out_shape,
        compiler_params=_cparams(),
        name="inproj_ab",
    )(x_p, x_s, g, w, w_lo, wt)


def _sb_prompt_kernel(bias_ref, q_ref, kt_ref, v_ref, o_ref, carry_ref, acc_ref):
    qi = pl.program_id(1)
    tq = q_ref.shape[0]
    lane = lax.broadcasted_iota(I32, (tq, LANES), 1)
    low_half = lane < SB_HEAD_DIM
    r = lax.broadcasted_iota(I32, (tq, tq), 0)
    c = lax.broadcasted_iota(I32, (tq, tq), 1)
    tri = jnp.where(r >= c, 1.0, 0.0).astype(BF16)
    causal = c < r

    carry_ref[...] = jnp.zeros_like(carry_ref)
    acc_ref[...] = jnp.zeros_like(acc_ref)

    def block(kj, diagonal):
        ks = pl.multiple_of(kj * tq, tq)
        for h in range(SB_HEADS):
            p, half = divmod(h, 2)
            lanes = slice(p * LANES, (p + 1) * LANES)
            q_pair = q_ref[:, lanes]
            q_h = jnp.where(low_half if half == 0 else jnp.logical_not(low_half), q_pair, jnp.zeros_like(q_pair))
            kt = kt_ref[0, lanes, pl.ds(ks, tq)]
            z = _dot(q_h, kt) + bias_ref[h]
            sp = _softplus(z)
            if diagonal:
                sp = jnp.where(causal, sp, 0.0)
            carry = carry_ref[h]
            arg = z + jnp.concatenate([carry] * (tq // LANES), axis=1) - _dot(sp.astype(BF16), tri)
            if diagonal:
                arg = jnp.where(causal, arg, NEG_BIG)
            w = jnp.exp(arg).astype(BF16)
            acc_ref[h] += _dot(w, v_ref[pl.ds(ks, tq), lanes])
            carry_ref[h] = carry - jnp.broadcast_to(jnp.sum(sp, axis=1, keepdims=True), carry.shape)

    block(qi, True)

    def body(j, _):
        block(qi - 1 - j, False)
        return 0

    lax.fori_loop(0, qi, body, 0)

    for p in range(SB_HEADS // 2):
        o_ref[:, p * LANES:(p + 1) * LANES] = jnp.where(low_half, acc_ref[2 * p], acc_ref[2 * p + 1]).astype(o_ref.dtype)


def _sb_prompt(bias, q, kbt, vb, *, batch, seq):
    n, d_sb = q.shape
    nq = seq // TQ
    return pl.pallas_call(
        _sb_prompt_kernel,
        grid=(batch, nq),
        in_specs=[
            pl.BlockSpec(memory_space=pltpu.SMEM),
            pl.BlockSpec((TQ, d_sb), lambda b, i: (b * nq + i, 0)),
            pl.BlockSpec((1, d_sb, seq), lambda b, i: (b, 0, 0)),
            pl.BlockSpec((seq, d_sb), lambda b, i: (b, 0)),
        ],
        out_specs=pl.BlockSpec((TQ, d_sb), lambda b, i: (b * nq + i, 0)),
        out_shape=jax.ShapeDtypeStruct((n, d_sb), BF16),
        scratch_shapes=[pltpu.VMEM((SB_HEADS, TQ, LANES), F32), pltpu.VMEM((SB_HEADS, TQ, LANES), F32)],
        compiler_params=_cparams(2),
        name="sb_prompt",
    )(bias, q, kbt, vb)


def _sb_sample_kernel(pt_ref, q_ref, bias_ref, *refs, pp):
    del pt_ref
    k_refs, v_refs = refs[:pp], refs[pp:2 * pp]
    o_ref, carry_ref, acc_ref = refs[2 * pp:]
    j = pl.program_id(1)
    d_sb = SB_HEADS * SB_HEAD_DIM
    page = k_refs[0].shape[-1]

    @pl.when(j == 0)
    def _():
        carry_ref[...] = jnp.zeros_like(carry_ref)
        acc_ref[...] = jnp.zeros_like(acc_ref)

    head_of_lane = lax.broadcasted_iota(I32, (SB_HEADS, d_sb), 1) // SB_HEAD_DIM
    row = lax.broadcasted_iota(I32, (SB_HEADS, d_sb), 0)
    own = head_of_lane == row
    q_rows = jnp.broadcast_to(q_ref[0], (SB_HEADS, d_sb))
    q_bd = jnp.where(own, q_rows, 0.0).astype(BF16)
    r = lax.broadcasted_iota(I32, (page, page), 0)
    c = lax.broadcasted_iota(I32, (page, page), 1)
    tri = jnp.where(r >= c, 1.0, 0.0).astype(BF16)

    for pg in reversed(range(pp)):
        kt = k_refs[pg][0].reshape(d_sb, page).astype(BF16)
        vt = v_refs[pg][0].reshape(d_sb, page).astype(BF16)
        z = _dot(q_bd, kt) + bias_ref[...]
        sp = _softplus(z)
        carry = carry_ref[...]
        w = jnp.exp(z + carry - _dot(sp.astype(BF16), tri)).astype(BF16)
        acc_ref[...] += _dot_nt(w, vt)
        carry_ref[...] = carry - jnp.broadcast_to(jnp.sum(sp, axis=1, keepdims=True), carry.shape)

    @pl.when(j == pl.num_programs(1) - 1)
    def _():
        o_ref[0] = jnp.sum(jnp.where(own, acc_ref[...], 0.0), axis=0, keepdims=True)


def _sb_sample(page_table, q3, bias_b, kt_pages, vt_pages):
    db, n_pages = page_table.shape
    d_sb = q3.shape[-1]
    page = kt_pages.shape[-1]
    pp = PAGES_PER_STEP
    assert n_pages % pp == 0
    nchunk = n_pages // pp

    def page_spec(pg):
        return pl.BlockSpec((1, SB_HEADS, SB_HEAD_DIM, page),
                            lambda b, j, pt: (pt[b, (nchunk - 1 - j) * pp + pg], 0, 0, 0))

    grid_spec = pltpu.PrefetchScalarGridSpec(
        num_scalar_prefetch=1,
        grid=(db, nchunk),
        in_specs=[pl.BlockSpec((1, 1, d_sb), lambda b, j, pt: (b, 0, 0)),
                  pl.BlockSpec((SB_HEADS, page), lambda b, j, pt: (0, 0))]
        + [page_spec(pg) for pg in range(pp)] * 2,
        out_specs=pl.BlockSpec((1, 1, d_sb), lambda b, j, pt: (b, 0, 0)),
        scratch_shapes=[pltpu.VMEM((SB_HEADS, page), F32), pltpu.VMEM((SB_HEADS, d_sb), F32)],
    )
    return pl.pallas_call(
        functools.partial(_sb_sample_kernel, pp=pp),
        grid_spec=grid_spec,
        out_shape=jax.ShapeDtypeStruct((db, 1, d_sb), F32),
        compiler_params=_cparams(2),
        name="sb_sample",
    )(page_table, q3, bias_b, *([kt_pages] * pp), *([vt_pages] * pp))


def _route(hn2, wr_hi_ref, wr_lo_ref, rb_ref, dense):
    m = hn2.shape[0]
    x_hi, x_lo = _split(hn2)
    w_hi, w_lo = wr_hi_ref[...], wr_lo_ref[...]
    lg = (_dot_nt(w_hi, x_hi) + _dot_nt(w_lo, x_hi)) + (_dot_nt(w_hi, x_lo) + _dot_nt(w_lo, x_lo)) + rb_ref[...]

    g = [lg[k:k + 1, :] for k in range(N_EXPERT_GROUPS)]
    gmax = functools.reduce(jnp.maximum, g)
    gidx = jnp.full((1, m), N_EXPERT_GROUPS - 1, I32)
    for k in reversed(range(N_EXPERT_GROUPS - 1)):
        gidx = jnp.where(g[k] == gmax, k, gidx)
    p_group = 1.0 / functools.reduce(lambda a, b: a + b, [jnp.exp(gk - gmax) for gk in g])

    e = []
    for k in range(EXPERTS_PER_GROUP):
        last = 8 + (N_EXPERT_GROUPS - 1) * EXPERTS_PER_GROUP + k
        ek = lg[last:last + 1, :]
        for gi in reversed(range(N_EXPERT_GROUPS - 1)):
            row = 8 + gi * EXPERTS_PER_GROUP + k
            ek = jnp.where(gidx == gi, lg[row:row + 1, :], ek)
        e.append(ek)
    v1 = functools.reduce(jnp.maximum, e)
    i1 = jnp.full((1, m), EXPERTS_PER_GROUP - 1, I32)
    for k in reversed(range(EXPERTS_PER_GROUP - 1)):
        i1 = jnp.where(e[k] == v1, k, i1)
    e2 = [jnp.where(i1 == k, -jnp.inf, e[k]) for k in range(EXPERTS_PER_GROUP)]
    v2 = functools.reduce(jnp.maximum, e2)
    i2 = jnp.full((1, m), EXPERTS_PER_GROUP - 1, I32)
    for k in reversed(range(EXPERTS_PER_GROUP - 1)):
        i2 = jnp.where(e2[k] == v2, k, i2)
    t = jnp.exp(v2 - v1)
    w1 = p_group / (1.0 + t)
    w2 = p_group * t / (1.0 + t)
    rows = lax.broadcasted_iota(I32, (LANES, m), 0)

    if dense:
        e1 = gidx * EXPERTS_PER_GROUP + i1
        e2id = gidx * EXPERTS_PER_GROUP + i2
        coef_t = jnp.where(rows == e1, w1, 0.0) + jnp.where(rows == e2id, w2, 0.0)
        return None, coef_t.T

    first_low = i1 < i2
    ia = jnp.where(first_low, i1, i2)
    ib = jnp.where(first_low, i2, i1)
    ca = jnp.where(first_low, w1, w2)
    cb = jnp.where(first_low, w2, w1)
    pidx = jnp.zeros((1, m), I32)
    for n, (pa, pb) in enumerate(PAIRS):
        pidx = jnp.where(jnp.logical_and(ia == pa, ib == pb), n, pidx)
    bucket = gidx * len(PAIRS) + pidx
    coef_t = jnp.where(rows == 0, ca, jnp.where(rows == 1, cb, 0.0))
    return bucket, coef_t.T


def _ffn_norm_route_store(h1, gf_ref, wr_hi_ref, wr_lo_ref, rb_ref, h1_ref, xe_ref, bk_ref):
    d = h1.shape[1]
    hn2 = _rms(h1, gf_ref[...])
    bucket, coef = _route(hn2, wr_hi_ref, wr_lo_ref, rb_ref, dense=bk_ref is None)
    h1_ref[...] = h1
    xe_ref[:, :d] = hn2
    xe_ref[:, d:] = coef
    if bk_ref is not None:
        bk_ref[...] = bucket


def _pool_windows(xp_ref, inv_cnt):
    m = xp_ref.shape[0] - HALO_POOL
    gd = xp_ref.shape[1] // len(POOL_WINDOWS)
    outs = []
    for gi, w in enumerate(POOL_WINDOWS):
        lanes = slice(gi * gd, (gi + 1) * gd)
        s = xp_ref[HALO_POOL:HALO_POOL + m, lanes]
        cur = s
        for j in range(1, w):
            s = s + xp_ref[HALO_POOL - j:HALO_POOL - j + m, lanes]
        outs.append(s * inv_cnt(w) - cur)
    return outs


def _mix_ab_kernel(hp_ref, osbp_ref, up_ref, uprev_ref, hs_ref, osbs_ref, us_ref, hist_ref,
                   wpool_ref, wpool_lo_ref, pscale_ref, wout_ref, wout_lo_ref, gf_ref, wr_hi_ref, wr_lo_ref, rb_ref,
                   h1p_ref, xep_ref, bkp_ref, h1s_ref, xes_ref, xp_ref, *, npt, tps, pos0):
    i = pl.program_id(0)
    d_sb = osbp_ref.shape[1]

    def tail(pooled, osb, h, precise):
        o_pool = jnp.concatenate(
            [_dot3(*_split(pooled[gi]), wpool_ref[gi], wpool_lo_ref[gi]) for gi in range(len(POOL_WINDOWS))],
            axis=1) * pscale_ref[...]
        mix = (_mm(osb, wout_ref, wout_lo_ref, precise, rows=slice(0, d_sb))
               + _mm(o_pool, wout_ref, wout_lo_ref, True, rows=slice(d_sb, None)))
        return h + mix

    @pl.when(i < npt)
    def _():
        tin = i % tps
        xp_ref[:HALO_POOL, :] = jnp.where(tin == 0, 0.0, uprev_ref[...])
        xp_ref[HALO_POOL:, :] = up_ref[...]
        pos = tin * TM + lax.broadcasted_iota(I32, (TM, 1), 0)

        def inv_cnt(w):
            return 1.0 / jnp.minimum(pos + 1, w).astype(F32)

        h1 = tail(_pool_windows(xp_ref, inv_cnt), osbp_ref[...], hp_ref[...], False)
        _ffn_norm_route_store(h1, gf_ref, wr_hi_ref, wr_lo_ref, rb_ref, h1p_ref, xep_ref, bkp_ref)

    @pl.when(i == npt)
    def _():
        gd = us_ref.shape[1] // len(POOL_WINDOWS)
        db = hist_ref.shape[1]
        u = us_ref[...]
        pooled = []
        for gi, w in enumerate(POOL_WINDOWS):
            lanes = slice(gi * gd, (gi + 1) * gd)
            s = u[:db, lanes]
            for j in range(1, w):
                s = s + hist_ref[POOL_HIST - j][:, lanes]
            s = jnp.concatenate([s, u[db:, lanes]], axis=0)
            pooled.append(s * (1.0 / min(pos0 + 1, w)) - u[:, lanes])
        h1 = tail(pooled, osbs_ref[...], hs_ref[...], True)
        _ffn_norm_route_store(h1, gf_ref, wr_hi_ref, wr_lo_ref, rb_ref, h1s_ref, xes_ref, None)


def _mix_ab(h_p, osb_p, u_p, h_s, osb_s, u_s, hist, wpool, wpool_lo, pscale, wout, wout_lo, gf, wr_hi, wr_lo, rb, *, seq, pos0):
    n, d = h_p.shape
    npt = n // TM
    tps = seq // TM
    d_sb = osb_p.shape[1]
    d_pool = u_p.shape[1]
    hb = TM // HALO_POOL
    consts = (wpool, wpool_lo, pscale, wout, wout_lo, gf, wr_hi, wr_lo, rb)
    in_specs = [
        _pspec(d, npt), _pspec(d_sb, npt), _pspec(d_pool, npt),
        pl.BlockSpec((HALO_POOL, d_pool), lambda i: (jnp.maximum(jnp.minimum(i, npt - 1) * hb - 1, 0), 0)),
        _const_spec((SP, d)), _const_spec((SP, d_sb)), _const_spec((SP, d_pool)), _const_spec(hist.shape),
    ] + [_const_spec(c.shape) for c in consts]
    de = d + LANES
    out_shape = (
        jax.ShapeDtypeStruct((n, d), F32), jax.ShapeDtypeStruct((n, de), F32), jax.ShapeDtypeStruct((1, n), I32),
        jax.ShapeDtypeStruct((SP, d), F32), jax.ShapeDtypeStruct((SP, de), F32),
    )
    out_specs = (
        _pspec(d, npt), _pspec(de, npt), pl.BlockSpec((1, TM), lambda i: (0, jnp.minimum(i, npt - 1))),
        _const_spec((SP, d)), _const_spec((SP, de)),
    )
    return pl.pallas_call(
        functools.partial(_mix_ab_kernel, npt=npt, tps=tps, pos0=pos0),
        grid=(npt + 1,),
        in_specs=in_specs, out_specs=out_specs, out_shape=out_shape,
        scratch_shapes=[pltpu.VMEM((HALO_POOL + TM, d_pool), F32)],
        compiler_params=_cparams(),
        name="mix_ab",
    )(h_p, osb_p, u_p, u_p, h_s, osb_s, u_s, hist, *consts)


def _sort_kernel(b_ref, pos_ref, tiles_ref, carry_ref, start_ref):
    ps = pl.program_id(0)
    j = pl.program_id(1)
    cw = b_ref.shape[1]
    rows = lax.broadcasted_iota(I32, (NB_ROWS, cw), 0)
    onehot = rows == b_ref[...]
    oh = jnp.where(onehot, 1.0, 0.0)
    counts = jnp.broadcast_to(jnp.sum(oh, axis=1, keepdims=True), (NB_ROWS, LANES))

    @pl.when(jnp.logical_and(ps == 0, j == 0))
    def _():
        carry_ref[...] = jnp.zeros_like(carry_ref)

    @pl.when(ps == 0)
    def _():
        carry_ref[...] += counts

    @pl.when(jnp.logical_and(ps == 1, j == 0))
    def _():
        real = lax.broadcasted_iota(I32, (NB_ROWS, LANES), 0) < N_BUCKETS
        tiles = jnp.where(real, jnp.floor((carry_ref[...] + (TMX - 1)) * (1.0 / TMX)), 0.0)
        rr = lax.broadcasted_iota(I32, (NB_ROWS, NB_ROWS), 0)
        cc = lax.broadcasted_iota(I32, (NB_ROWS, NB_ROWS), 1)
        below = jnp.where(cc < rr, 1.0, 0.0).astype(BF16)
        start_ref[...] = _dot(below, tiles.astype(BF16)) * TMX
        tiles_ref[...] = tiles
        carry_ref[...] = jnp.zeros_like(carry_ref)

    @pl.when(ps == 1)
    def _():
        rj = lax.broadcasted_iota(I32, (cw, cw), 0)
        ct = lax.broadcasted_iota(I32, (cw, cw), 1)
        before = jnp.where(rj < ct, 1.0, 0.0).astype(BF16)
        rank = _dot(oh.astype(BF16), before)
        val = rank + carry_ref[:, 0:1] + start_ref[:, 0:1]
        pos_ref[...] = jnp.sum(jnp.where(onehot, val, 0.0), axis=0, keepdims=True).astype(I32)
        carry_ref[...] += counts


def _sort_positions(buckets):
    npad = buckets.shape[1]
    nch = npad // SORT_CHUNK
    return pl.pallas_call(
        _sort_kernel,
        grid=(2, nch),
        in_specs=[pl.BlockSpec((1, SORT_CHUNK), lambda ps, j: (0, j))],
        out_specs=(pl.BlockSpec((1, SORT_CHUNK), lambda ps, j: (0, j * ps)),
                   pl.BlockSpec((NB_ROWS, LANES), lambda ps, j: (0, 0))),
        out_shape=(jax.ShapeDtypeStruct((1, npad), I32), jax.ShapeDtypeStruct((NB_ROWS, LANES), F32)),
        scratch_shapes=[pltpu.VMEM((NB_ROWS, LANES), F32), pltpu.VMEM((NB_ROWS, LANES), F32)],
        compiler_params=_cparams(2),
        name="moe_sort",
    )(buckets)


def _move_rows_kernel(pos_ref, src_ref, *refs, scatter, chunk):
    dst_ref, sems = refs[-2:]
    base = pl.program_id(0) * chunk

    def copy(r):
        t, p = base + r, pos_ref[0, r]
        s_row, d_row = (t, p) if scatter else (p, t)
        return pltpu.make_async_copy(src_ref.at[pl.ds(s_row, 1)], dst_ref.at[pl.ds(d_row, 1)], sems.at[lax.rem(r, DMA_RING)])

    def issue(r, _):
        @pl.when(r >= DMA_RING)
        def _():
            copy(r - DMA_RING).wait()

        copy(r).start()
        return 0

    lax.fori_loop(0, chunk, issue, 0)

    def drain(r, _):
        copy(r).wait()
        return 0

    lax.fori_loop(chunk - DMA_RING, chunk, drain, 0)


def _scatter_rows(pos, src, dst, *, chunk):
    n = pos.shape[1]
    assert n % chunk == 0 and chunk >= DMA_RING
    return pl.pallas_call(
        functools.partial(_move_rows_kernel, scatter=True, chunk=chunk),
        grid=(n // chunk,),
        in_specs=[pl.BlockSpec((1, chunk), lambda i: (0, i), memory_space=pltpu.SMEM),
                  pl.BlockSpec(memory_space=pl.ANY), pl.BlockSpec(memory_space=pl.ANY)],
        out_specs=pl.BlockSpec(memory_space=pl.ANY),
        out_shape=jax.ShapeDtypeStruct(dst.shape, dst.dtype),
        scratch_shapes=[pltpu.SemaphoreType.DMA((DMA_RING,))],
        input_output_aliases={2: 0},
        compiler_params=_cparams(),
        name="moe_scatter",
    )(pos, src, dst)


def _gather_rows(pos, src, *, chunk):
    n = pos.shape[1]
    assert n % chunk == 0 and chunk >= DMA_RING
    return pl.pallas_call(
        functools.partial(_move_rows_kernel, scatter=False, chunk=chunk),
        grid=(n // chunk,),
        in_specs=[pl.BlockSpec((1, chunk), lambda i: (0, i), memory_space=pltpu.SMEM),
                  pl.BlockSpec(memory_space=pl.ANY)],
        out_specs=pl.BlockSpec(memory_space=pl.ANY),
        out_shape=jax.ShapeDtypeStruct((n, src.shape[1]), src.dtype),
        scratch_shapes=[pltpu.SemaphoreType.DMA((DMA_RING,))],
        compiler_params=_cparams(),
        name="moe_gather",
    )(pos, src)


def _experts_kernel(ea_ref, eb_ref, valid_ref, xs_ref, wga_ref, wua_ref, wda_ref, wgb_ref, wub_ref, wdb_ref, ys_ref):
    i = pl.program_id(0)
    d = ys_ref.shape[1]

    @pl.when(valid_ref[i] == 1)
    def _():
        x = xs_ref[:, :d].astype(BF16)

        def expert(wg_ref, wu_ref, wd_ref):
            g = _dot(x, wg_ref[0].astype(BF16))
            u = _dot(x, wu_ref[0].astype(BF16))
            hid = g * _sigmoid(g) * u
            return _dot(hid.astype(BF16), wd_ref[0].astype(BF16))

        ya = expert(wga_ref, wua_ref, wda_ref)
        yb = expert(wgb_ref, wub_ref, wdb_ref)
        ys_ref[...] = xs_ref[:, d:d + 1] * ya + xs_ref[:, d + 1:d + 2] * yb

    @pl.when(valid_ref[i] == 0)
    def _():
        ys_ref[...] = jnp.zeros_like(ys_ref)


def _experts(ea, eb, valid, xs, w_gate, w_up, w_down):
    ns, de = xs.shape
    d = de - LANES
    dx = w_gate.shape[-1]
    ntiles = ns // TMX

    def wspec(shape, which):
        return pl.BlockSpec((1,) + shape, lambda i, ea, eb, valid: ((ea, eb)[which][i], 0, 0))

    grid_spec = pltpu.PrefetchScalarGridSpec(
        num_scalar_prefetch=3,
        grid=(ntiles,),
        in_specs=[pl.BlockSpec((TMX, de), lambda i, ea, eb, valid: (i, 0)),
                  wspec((d, dx), 0), wspec((d, dx), 0), wspec((dx, d), 0),
                  wspec((d, dx), 1), wspec((d, dx), 1), wspec((dx, d), 1)],
        out_specs=pl.BlockSpec((TMX, d), lambda i, ea, eb, valid: (i, 0)),
    )
    return pl.pallas_call(
        _experts_kernel,
        grid_spec=grid_spec,
        out_shape=jax.ShapeDtypeStruct((ns, d), F32),
        compiler_params=_cparams(),
        name="moe_experts",
    )(ea, eb, valid, xs, w_gate, w_up, w_down, w_gate, w_up, w_down)


def _moe_sparse(xe, bk, w_gate, w_up, w_down):
    n_tok, de = xe.shape
    npad = -(-n_tok // SORT_CHUNK) * SORT_CHUNK
    buckets = jnp.concatenate([bk, jnp.full((1, npad - n_tok), N_BUCKETS, I32)], axis=1)
    pos, tiles = _sort_positions(buckets)

    ntiles = -(-n_tok // TMX) + N_BUCKETS
    tiles_per_bucket = tiles[:N_BUCKETS, 0].astype(I32)
    cum = jnp.cumsum(tiles_per_bucket)
    tile_ids = jnp.arange(ntiles, dtype=I32)
    valid = tile_ids < cum[-1]
    tb = jnp.searchsorted(cum, jnp.minimum(tile_ids, cum[-1] - 1), side="right").astype(I32)
    group, pidx = tb // len(PAIRS), tb % len(PAIRS)
    pa = jnp.asarray([p[0] for p in PAIRS], I32)[pidx]
    pb = jnp.asarray([p[1] for p in PAIRS], I32)[pidx]
    ea = group * EXPERTS_PER_GROUP + pa
    eb = group * EXPERTS_PER_GROUP + pb

    pos = pos[:, :n_tok]
    xs = _scatter_rows(pos, xe, jnp.zeros((ntiles * TMX, de), F32), chunk=TM)
    ys = _experts(ea, eb, valid.astype(I32), xs, w_gate, w_up, w_down)
    return _gather_rows(pos, ys, chunk=TM)


def _experts_dense_kernel(xe_ref, wg_ref, wu_ref, wd_ref, y_ref):
    e = pl.program_id(0)
    d = y_ref.shape[1]

    @pl.when(e == 0)
    def _():
        y_ref[...] = jnp.zeros_like(y_ref)

    x_hi, x_lo = _split(xe_ref[:, :d])
    g = _dot3(x_hi, x_lo, *_split(wg_ref[0]))
    u = _dot3(x_hi, x_lo, *_split(wu_ref[0]))
    hid = g * _sigmoid(g) * u
    y = _dot3(*_split(hid), *_split(wd_ref[0]))
    lane = lax.broadcasted_iota(I32, (xe_ref.shape[0], LANES), 1)
    coef = jnp.sum(jnp.where(lane == e, xe_ref[:, d:], 0.0), axis=1, keepdims=True)
    y_ref[...] += coef * y


def _moe_dense(xe, w_gate, w_up, w_down):
    m, de = xe.shape
    d = de - LANES
    n_exp, _, dx = w_gate.shape
    return pl.pallas_call(
        _experts_dense_kernel,
        grid=(n_exp,),
        in_specs=[_const_spec((m, de)),
                  pl.BlockSpec((1, d, dx), lambda e: (e, 0, 0)), pl.BlockSpec((1, d, dx), lambda e: (e, 0, 0)),
                  pl.BlockSpec((1, dx, d), lambda e: (e, 0, 0))],
        out_specs=_const_spec((m, d)),
        out_shape=jax.ShapeDtypeStruct((m, d), F32),
        compiler_params=_cparams(),
        name="moe_dense",
    )(xe, w_gate, w_up, w_down)


def _ple(h, y, p, gpg_ref, wpg_ref, wpg_lo_ref, bpg_ref, wple_ref, wple_lo_ref, precise):
    h2 = h + y
    gate = _sigmoid(_mm(_rms(h2, gpg_ref[...]), wpg_ref, wpg_lo_ref, precise) + bpg_ref[...])
    return h2 + gate * _mm(p, wple_ref, wple_lo_ref, precise)


def _ple_inproj_c_kernel(hp_ref, yp_ref, pp_ref, hs_ref, ys_ref, ps_ref,
                         gpg_ref, wpg_ref, wpg_lo_ref, bpg_ref, wple_ref, wple_lo_ref, gm_ref, win_ref, win_lo_ref,
                         h3p_ref, ucp_ref, h3s_ref, ucs_ref, *, npt):
    i = pl.program_id(0)

    def run(h_ref, y_ref, p_ref, h3_ref, uc_ref, precise):
        h3 = _ple(h_ref[...], y_ref[...], p_ref[...], gpg_ref, wpg_ref, wpg_lo_ref, bpg_ref, wple_ref, wple_lo_ref, precise)
        h3_ref[...] = h3
        ag = _mm(_rms(h3, gm_ref[...]), win_ref, win_lo_ref, precise)
        dc = ag.shape[1] // 2
        uc_ref[...] = ag[:, :dc] * _sigmoid(ag[:, dc:])

    @pl.when(i < npt)
    def _():
        run(hp_ref, yp_ref, pp_ref, h3p_ref, ucp_ref, False)

    @pl.when(i == npt)
    def _():
        run(hs_ref, ys_ref, ps_ref, h3s_ref, ucs_ref, True)


def _ple_inproj_c(h_p, y_p, p_p, h_s, y_s, p_s, gpg, wpg, wpg_lo, bpg, wple, wple_lo, gm, win, win_lo):
    n, d = h_p.shape
    npt = n // TM
    dple = p_p.shape[1]
    dc = win.shape[1] // 2
    consts = (gpg, wpg, wpg_lo, bpg, wple, wple_lo, gm, win, win_lo)
    return pl.pallas_call(
        functools.partial(_ple_inproj_c_kernel, npt=npt),
        grid=(npt + 1,),
        in_specs=[_pspec(d, npt), _pspec(d, npt), _pspec(dple, npt),
                  _const_spec((SP, d)), _const_spec((SP, d)), _const_spec((SP, dple))]
        + [_const_spec(c.shape) for c in consts],
        out_specs=(_pspec(d, npt), _pspec(dc, npt), _const_spec((SP, d)), _const_spec((SP, dc))),
        out_shape=(jax.ShapeDtypeStruct((n, d), F32), jax.ShapeDtypeStruct((n, dc), F32),
                   jax.ShapeDtypeStruct((SP, d), F32), jax.ShapeDtypeStruct((SP, dc), F32)),
        compiler_params=_cparams(),
        name="ple_inproj_c",
    )(h_p, y_p, p_p, h_s, y_s, p_s, *consts)


def _ple_final_kernel(hp_ref, yp_ref, pp_ref, hs_ref, ys_ref, ps_ref,
                      gpg_ref, wpg_ref, wpg_lo_ref, bpg_ref, wple_ref, wple_lo_ref, gfin_ref, op_ref, os_ref, *, npt):
    i = pl.program_id(0)

    def run(h_ref, y_ref, p_ref, o_ref, precise):
        h = _ple(h_ref[...], y_ref[...], p_ref[...], gpg_ref, wpg_ref, wpg_lo_ref, bpg_ref, wple_ref, wple_lo_ref, precise)
        o_ref[...] = _rms(h, gfin_ref[...])

    @pl.when(i < npt)
    def _():
        run(hp_ref, yp_ref, pp_ref, op_ref, False)

    @pl.when(i == npt)
    def _():
        run(hs_ref, ys_ref, ps_ref, os_ref, True)


def _ple_final(h_p, y_p, p_p, h_s, y_s, p_s, gpg, wpg, wpg_lo, bpg, wple, wple_lo, gfin):
    n, d = h_p.shape
    npt = n // TM
    dple = p_p.shape[1]
    consts = (gpg, wpg, wpg_lo, bpg, wple, wple_lo, gfin)
    return pl.pallas_call(
        functools.partial(_ple_final_kernel, npt=npt),
        grid=(npt + 1,),
        in_specs=[_pspec(d, npt), _pspec(d, npt), _pspec(dple, npt),
                  _const_spec((SP, d)), _const_spec((SP, d)), _const_spec((SP, dple))]
        + [_const_spec(c.shape) for c in consts],
        out_specs=(_pspec(d, npt), _const_spec((SP, d))),
        out_shape=(jax.ShapeDtypeStruct((n, d), F32), jax.ShapeDtypeStruct((SP, d), F32)),
        compiler_params=_cparams(),
        name="ple_final",
    )(h_p, y_p, p_p, h_s, y_s, p_s, *consts)


def _conv_tail(y, h, bdw_ref, lng_ref, lnb_ref, wout_ref, wout_lo_ref, precise):
    y = y + bdw_ref[...]
    mu = jnp.mean(y, axis=-1, keepdims=True)
    yc = y - mu
    yn = yc * lax.rsqrt(jnp.mean(yc * yc, axis=-1, keepdims=True) + EPS) * lng_ref[...] + lnb_ref[...]
    act = yn * _sigmoid(yn)
    return h + _mm(act, wout_ref, wout_lo_ref, precise)


def _mix_c_kernel(hp_ref, ucp_ref, ucprev_ref, hs_ref, ucs_ref, hist_ref,
                  wdw_ref, bdw_ref, lng_ref, lnb_ref, wout_ref, wout_lo_ref, gf_ref, wr_hi_ref, wr_lo_ref, rb_ref,
                  h1p_ref, xep_ref, bkp_ref, h1s_ref, xes_ref, xp_ref, *, npt, tps):
    i = pl.program_id(0)

    @pl.when(i < npt)
    def _():
        tin = i % tps
        xp_ref[:HALO_CONV, :] = jnp.where(tin == 0, 0.0, ucprev_ref[...])
        xp_ref[HALO_CONV:, :] = ucp_ref[...]
        off = HALO_CONV - CONV_HIST
        y = wdw_ref[0:1, :] * xp_ref[off:off + TM, :]
        for j in range(1, CONV_WIDTH):
            y = y + wdw_ref[j:j + 1, :] * xp_ref[off + j:off + j + TM, :]
        h1 = _conv_tail(y, hp_ref[...], bdw_ref, lng_ref, lnb_ref, wout_ref, wout_lo_ref, False)
        _ffn_norm_route_store(h1, gf_ref, wr_hi_ref, wr_lo_ref, rb_ref, h1p_ref, xep_ref, bkp_ref)

    @pl.when(i == npt)
    def _():
        db = hist_ref.shape[1]
        y = wdw_ref[CONV_WIDTH - 1:CONV_WIDTH, :] * ucs_ref[...]
        y_hist = wdw_ref[0:1, :] * hist_ref[0]
        for j in range(1, CONV_HIST):
            y_hist = y_hist + wdw_ref[j:j + 1, :] * hist_ref[j]
        y = jnp.concatenate([y[:db] + y_hist, y[db:]], axis=0)
        h1 = _conv_tail(y, hs_ref[...], bdw_ref, lng_ref, lnb_ref, wout_ref, wout_lo_ref, True)
        _ffn_norm_route_store(h1, gf_ref, wr_hi_ref, wr_lo_ref, rb_ref, h1s_ref, xes_ref, None)


def _mix_c(h_p, uc_p, h_s, uc_s, hist, wdw, bdw, lng, lnb, wout, wout_lo, gf, wr_hi, wr_lo, rb, *, seq):
    n, d = h_p.shape
    npt = n // TM
    tps = seq // TM
    dc = uc_p.shape[1]
    hb = TM // HALO_CONV
    consts = (wdw, bdw, lng, lnb, wout, wout_lo, gf, wr_hi, wr_lo, rb)
    in_specs = [
        _pspec(d, npt), _pspec(dc, npt),
        pl.BlockSpec((HALO_CONV, dc), lambda i: (jnp.maximum(jnp.minimum(i, npt - 1) * hb - 1, 0), 0)),
        _const_spec((SP, d)), _const_spec((SP, dc)), _const_spec(hist.shape),
    ] + [_const_spec(c.shape) for c in consts]
    de = d + LANES
    out_shape = (
        jax.ShapeDtypeStruct((n, d), F32), jax.ShapeDtypeStruct((n, de), F32), jax.ShapeDtypeStruct((1, n), I32),
        jax.ShapeDtypeStruct((SP, d), F32), jax.ShapeDtypeStruct((SP, de), F32),
    )
    out_specs = (
        _pspec(d, npt), _pspec(de, npt), pl.BlockSpec((1, TM), lambda i: (0, jnp.minimum(i, npt - 1))),
        _const_spec((SP, d)), _const_spec((SP, de)),
    )
    return pl.pallas_call(
        functools.partial(_mix_c_kernel, npt=npt, tps=tps),
        grid=(npt + 1,),
        in_specs=in_specs, out_specs=out_specs, out_shape=out_shape,
        scratch_shapes=[pltpu.VMEM((HALO_CONV + TM, dc), F32)],
        compiler_params=_cparams(),
        name="mix_c",
    )(h_p, uc_p, uc_p, h_s, uc_s, hist, *consts)


def _router_weights(w_rg, b_rg, w_re, b_re):
    d = w_rg.shape[0]
    wt = jnp.zeros((ROUTE_ROWS, d), F32).at[:N_EXPERT_GROUPS].set(w_rg.T).at[8:8 + w_re.shape[1]].set(w_re.T)
    hi, lo = _hi_lo(wt)
    rb = jnp.zeros((ROUTE_ROWS, 1), F32).at[:N_EXPERT_GROUPS, 0].set(b_rg).at[8:8 + w_re.shape[1], 0].set(b_re)
    return hi, lo, rb


def _pad_rows(x, rows):
    return jnp.concatenate([x, jnp.zeros((rows - x.shape[0],) + x.shape[1:], x.dtype)], axis=0)


def kernel(x_prompt, x_sample, p_prompt, p_sample, cache_k, cache_v, state_pool, state_conv, page_table, norm_mix, norm_ffn, norm_final, w_in_ab, sb_bias, w_pool, pool_scale, w_out_ab, w_in_c, w_dw, b_dw, ln_g_c, ln_b_c, w_out_c, w_rg, b_rg, w_re, b_re, w_gate_e, w_up_e, w_down_e, w_ple, w_ple_gate, b_ple_gate, g_ple_gate):
    batch, seq, d = x_prompt.shape
    db, dec_seq, _ = x_sample.shape
    depth = norm_mix.shape[0]
    d_sb = SB_HEADS * SB_HEAD_DIM
    n_pages = page_table.shape[1]
    page = cache_k.shape[2]
    pos0 = n_pages * page
    assert dec_seq == 1 and depth == 2 and db <= SP and db % 8 == 0 and w_gate_e.shape[1] == N_EXPERTS
    assert seq % TM == 0 and seq % TQ == 0 and TMX & (TMX - 1) == 0
    n_p = batch * seq

    row = lambda v: v.reshape(1, -1)
    x_p = x_prompt.reshape(n_p, d)
    x_s = _pad_rows(x_sample.reshape(db, d), SP)
    pp = [p_prompt[i].reshape(n_p, -1) for i in range(depth)]
    ps = [_pad_rows(p_sample[i].reshape(db, -1), SP) for i in range(depth)]

    w_in, w_in_lo = _hi_lo(w_in_ab[0])
    w_kv_t = w_in_ab[0][:, d_sb:3 * d_sb].T.astype(BF16)
    q_p, vb_p, u_p, kt_p, vt_p, kbt_p, q_s, u_s, k_s, v_s = _inproj_ab(
        x_p, x_s, row(norm_mix[0]), w_in, w_in_lo, w_kv_t, batch=batch, seq=seq)
    osb_p = _sb_prompt(sb_bias[0], q_p, kbt_p, vb_p, batch=batch, seq=seq)
    kt_pages = jnp.transpose(cache_k[0], (0, 2, 3, 1))
    vt_pages = jnp.transpose(cache_v[0], (0, 2, 3, 1))
    bias_b = jnp.broadcast_to(sb_bias[0][:, None], (SB_HEADS, page))
    osb_s = _sb_sample(page_table, q_s[:db].reshape(db, 1, d_sb), bias_b, kt_pages, vt_pages)
    osb_s = _pad_rows(osb_s.reshape(db, d_sb), SP)

    pool_hist = jnp.transpose(state_pool[0], (1, 0, 2))
    wr_hi, wr_lo, rb = _router_weights(w_rg[0], b_rg[0], w_re[0], b_re[0])
    w_pool_hi, w_pool_lo = _hi_lo(w_pool[0])
    w_out_hi, w_out_lo = _hi_lo(w_out_ab[0])
    h1_p, xe_p, bk_p, h1_s, xe_s = _mix_ab(
        x_p, osb_p, u_p, x_s, osb_s, u_s, pool_hist, w_pool_hi, w_pool_lo, row(pool_scale[0]),
        w_out_hi, w_out_lo, row(norm_ffn[0]), wr_hi, wr_lo, rb, seq=seq, pos0=pos0)
    y_p = _moe_sparse(xe_p, bk_p, w_gate_e[0], w_up_e[0], w_down_e[0])
    y_s = _moe_dense(xe_s, w_gate_e[0], w_up_e[0], w_down_e[0])

    w_pg, w_pg_lo = _hi_lo(w_ple_gate[0])
    w_pl, w_pl_lo = _hi_lo(w_ple[0])
    w_ic, w_ic_lo = _hi_lo(w_in_c[0])
    h3_p, uc_p, h3_s, uc_s = _ple_inproj_c(
        h1_p, y_p, pp[0], h1_s, y_s, ps[0], row(g_ple_gate[0]), w_pg, w_pg_lo, row(b_ple_gate[0]),
        w_pl, w_pl_lo, row(norm_mix[1]), w_ic, w_ic_lo)
    conv_hist = jnp.transpose(state_conv[0], (1, 0, 2))
    wr_hi, wr_lo, rb = _router_weights(w_rg[1], b_rg[1], w_re[1], b_re[1])
    w_oc, w_oc_lo = _hi_lo(w_out_c[0])
    h4_p, xe_p, bk_p, h4_s, xe_s = _mix_c(
        h3_p, uc_p, h3_s, uc_s, conv_hist, w_dw[0], row(b_dw[0]), row(ln_g_c[0]), row(ln_b_c[0]),
        w_oc, w_oc_lo, row(norm_ffn[1]), wr_hi, wr_lo, rb, seq=seq)
    y_p = _moe_sparse(xe_p, bk_p, w_gate_e[1], w_up_e[1], w_down_e[1])
    y_s = _moe_dense(xe_s, w_gate_e[1], w_up_e[1], w_down_e[1])
    w_pg, w_pg_lo = _hi_lo(w_ple_gate[1])
    w_pl, w_pl_lo = _hi_lo(w_ple[1])
    out_p, out_s = _ple_final(
        h4_p, y_p, pp[1], h4_s, y_s, ps[1], row(g_ple_gate[1]), w_pg, w_pg_lo, row(b_ple_gate[1]),
        w_pl, w_pl_lo, row(norm_final))

    def heads_last(t):
        return jnp.transpose(t.reshape(batch, SB_HEADS, SB_HEAD_DIM, seq), (0, 3, 1, 2))[None]

    y_prompt = out_p.reshape(batch, seq, d)
    y_sample = out_s[:db].reshape(db, 1, d)
    k_prompt, v_prompt = heads_last(kt_p), heads_last(vt_p)
    pool_prompt = u_p.reshape(batch, seq, -1)[:, seq - POOL_HIST:][None]
    conv_prompt = uc_p.reshape(batch, seq, -1)[:, seq - CONV_HIST:][None]
    k_sample = k_s[:db].reshape(1, db, 1, SB_HEADS, SB_HEAD_DIM)
    v_sample = v_s[:db].reshape(1, db, 1, SB_HEADS, SB_HEAD_DIM)
    pool_sample = jnp.transpose(jnp.concatenate([pool_hist[1:], u_s[None, :db]], axis=0), (1, 0, 2))[None]
    conv_sample = jnp.transpose(jnp.concatenate([conv_hist[1:], uc_s[None, :db]], axis=0), (1, 0, 2))[None]
    return (y_prompt, y_sample, k_prompt, v_prompt, pool_prompt, conv_prompt, k_sample, v_sample, pool_sample, conv_sample)
```

```python
import functools

import jax
import jax.numpy as jnp
from jax import lax
from jax.experimental import pallas as pl
from jax.experimental.pallas import tpu as pltpu

F32 = jnp.float32
BF16 = jnp.bfloat16
I32 = jnp.int32

EPS = 1e-6
SB_HEADS = 8
SB_HEAD_DIM = 64
POOL_WINDOWS = (2, 4, 8, 16)
POOL_HIST = max(POOL_WINDOWS) - 1
CONV_WIDTH = 31
CONV_HIST = CONV_WIDTH - 1
N_EXPERT_GROUPS = 4
EXPERTS_PER_GROUP = 4
N_EXPERTS = N_EXPERT_GROUPS * EXPERTS_PER_GROUP
PAIRS = ((0, 1), (0, 2), (0, 3), (1, 3), (1, 2), (2, 3))
N_BUCKETS = N_EXPERT_GROUPS * len(PAIRS)

LANES = 128
TM = 512
SP = 128
TQ = 256
TMX = 256
SORT_CHUNK = 512
NB_ROWS = 32
ROUTE_ROWS = 32
HALO_POOL = 16
HALO_CONV = 32
PAGES_PER_STEP = 16
MOVE_CHUNK = 2048
MOVE_BATCH = 128
MOVE_SEMS = 8
NEG_BIG = -1e30
VMEM_LIMIT = 56 * 1024 * 1024


def _cparams(n_axes=1, vmem=VMEM_LIMIT):
    return pltpu.CompilerParams(dimension_semantics=("arbitrary",) * n_axes, vmem_limit_bytes=vmem)


def _rms(x, g):
    return x * lax.rsqrt(jnp.mean(x * x, axis=-1, keepdims=True) + EPS) * g


def _sigmoid(x):
    return 1.0 / (1.0 + jnp.exp(-x))


def _softplus(z):
    return jnp.maximum(z, 0.0) + jnp.log(1.0 + jnp.exp(-jnp.abs(z)))


def _dot(a, b):
    return jnp.dot(a, b, preferred_element_type=F32)


def _dot_nt(a, b):
    return lax.dot_general(a, b, (((1,), (1,)), ((), ())), preferred_element_type=F32)


def _split(a):
    hi = a.astype(BF16)
    return hi, (a - hi.astype(F32)).astype(BF16)


def _dot3(a_hi, a_lo, w_hi, w_lo):
    return _dot(a_hi, w_hi) + (_dot(a_hi, w_lo) + _dot(a_lo, w_hi))


def _mm(a, w_ref, wlo_ref, precise, rows=slice(None), cols=slice(None)):
    if precise:
        return _dot3(*_split(a), w_ref[rows, cols], wlo_ref[rows, cols])
    return _dot(a.astype(BF16), w_ref[rows, cols])


def _hi_lo(w):
    hi = w.astype(BF16)
    return hi, (w - hi.astype(F32)).astype(BF16)


def _pspec(cols, npt, rows=TM, first_block=0):
    return pl.BlockSpec((rows, cols), lambda i: (first_block + jnp.minimum(i, npt - 1), 0))


def _const_spec(shape):
    nd = len(shape)
    return pl.BlockSpec(shape, lambda i: (0,) * nd)


def _inproj_ab_kernel(xp_ref, xs_ref, g_ref, w_ref, wlo_ref, wt_ref, wtlo_ref,
                      qhp_ref, qlp_ref, vhp_ref, vlp_ref, up_ref, ktp_ref, vtp_ref, khtp_ref, kltp_ref,
                      qs_ref, us_ref, ks_ref, vs_ref, *, npt, d_sb):
    i = pl.program_id(0)
    ucols = slice(3 * d_sb, None)

    @pl.when(i < npt)
    def _():
        hn, hn_lo = _split(_rms(xp_ref[...], g_ref[...]))
        def proj(cols):
            return _dot3(hn, hn_lo, w_ref[:, cols], wlo_ref[:, cols])

        qhp_ref[...], qlp_ref[...] = _split(proj(slice(0, d_sb)) * (SB_HEAD_DIM ** -0.5))
        vu = proj(slice(2 * d_sb, None))
        vhp_ref[...], vlp_ref[...] = _split(vu[:, :d_sb])
        up_ref[...] = vu[:, d_sb:]
        kvt = _dot_nt(wt_ref[...], hn) + (_dot_nt(wtlo_ref[...], hn) + _dot_nt(wt_ref[...], hn_lo))
        ktp_ref[0] = kvt[:d_sb]
        vtp_ref[0] = kvt[d_sb:]
        khtp_ref[0], kltp_ref[0] = _split(kvt[:d_sb])

    @pl.when(i == npt)
    def _():
        pr = _mm(_rms(xs_ref[...], g_ref[...]), w_ref, wlo_ref, True)
        qs_ref[...] = pr[:, :d_sb] * (SB_HEAD_DIM ** -0.5)
        ks_ref[...] = pr[:, d_sb:2 * d_sb]
        vs_ref[...] = pr[:, 2 * d_sb:3 * d_sb]
        us_ref[...] = pr[:, ucols]


def _inproj_ab(x_p, x_s, g, w, w_lo, wt, wt_lo, *, batch, seq):
    n, d = x_p.shape
    npt = n // TM
    tps = seq // TM
    d_sb = SB_HEADS * SB_HEAD_DIM
    d_pool = w.shape[1] - 3 * d_sb
    tspec = pl.BlockSpec((1, d_sb, TM), lambda i: (jnp.minimum(i, npt - 1) // tps, 0, jnp.minimum(i, npt - 1) % tps))
    out_shape = (
        jax.ShapeDtypeStruct((n, d_sb), BF16),
        jax.ShapeDtypeStruct((n, d_sb), BF16),
        jax.ShapeDtypeStruct((n, d_sb), BF16),
        jax.ShapeDtypeStruct((n, d_sb), BF16),
        jax.ShapeDtypeStruct((n, d_pool), F32),
        jax.ShapeDtypeStruct((batch, d_sb, seq), F32),
        jax.ShapeDtypeStruct((batch, d_sb, seq), F32),
        jax.ShapeDtypeStruct((batch, d_sb, seq), BF16),
        jax.ShapeDtypeStruct((batch, d_sb, seq), BF16),
        jax.ShapeDtypeStruct((SP, d_sb), F32),
        jax.ShapeDtypeStruct((SP, d_pool), F32),
        jax.ShapeDtypeStruct((SP, d_sb), F32),
        jax.ShapeDtypeStruct((SP, d_sb), F32),
    )
    out_specs = (
        _pspec(d_sb, npt), _pspec(d_sb, npt), _pspec(d_sb, npt), _pspec(d_sb, npt), _pspec(d_pool, npt),
        tspec, tspec, tspec, tspec,
        _const_spec((SP, d_sb)), _const_spec((SP, d_pool)), _const_spec((SP, d_sb)), _const_spec((SP, d_sb)),
    )
    consts = (g, w, w_lo, wt, wt_lo)
    return pl.pallas_call(
        functools.partial(_inproj_ab_kernel, npt=npt, d_sb=d_sb),
        grid=(npt + 1,),
        in_specs=[_pspec(d, npt), _const_spec((SP, d))] + [_const_spec(c.shape) for c in consts],
        out_specs=out_specs,
        out_shape=out_shape,
        compiler_params=_cparams(),
        name="inproj_ab",
    )(x_p, x_s, *consts)


def _sb_prompt_kernel(bias_ref, qh_ref, ql_ref, kth_ref, ktl_ref, vh_ref, vl_ref, o_ref, carry_ref, acc_ref):
    qi = pl.program_id(1)
    nq = pl.num_programs(1)
    tq = qh_ref.shape[0]
    lane = lax.broadcasted_iota(I32, (tq, LANES), 1)
    low_half = lane < SB_HEAD_DIM
    r = lax.broadcasted_iota(I32, (tq, tq), 0)
    c = lax.broadcasted_iota(I32, (tq, tq), 1)
    tri = jnp.where(r >= c, 1.0, 0.0).astype(BF16)
    causal = c < r

    carry_ref[...] = jnp.zeros_like(carry_ref)
    acc_ref[...] = jnp.zeros_like(acc_ref)

    def block(kj, diagonal, precise):
        ks = pl.multiple_of(kj * tq, tq)
        for h in range(SB_HEADS):
            p, half = divmod(h, 2)
            lanes = slice(p * LANES, (p + 1) * LANES)
            own = low_half if half == 0 else jnp.logical_not(low_half)

            def head_q(ref):
                q_pair = ref[:, lanes]
                return jnp.where(own, q_pair, jnp.zeros_like(q_pair))

            q_h = head_q(qh_ref)
            kt = kth_ref[0, lanes, pl.ds(ks, tq)]
            z = _dot(q_h, kt)
            if precise:
                z = z + (_dot(q_h, ktl_ref[0, lanes, pl.ds(ks, tq)]) + _dot(head_q(ql_ref), kt))
            z = z + bias_ref[h]
            sp = _softplus(z)
            if diagonal:
                sp = jnp.where(causal, sp, 0.0)
            carry = carry_ref[h]
            if precise:
                sp_hi, sp_lo = _split(sp)
                cum = _dot(sp_hi, tri) + _dot(sp_lo, tri)
            else:
                cum = _dot(sp.astype(BF16), tri)
            arg = z + jnp.concatenate([carry] * (tq // LANES), axis=1) - cum
            if diagonal:
                arg = jnp.where(causal, arg, NEG_BIG)
            w = jnp.exp(arg)
            v = vh_ref[pl.ds(ks, tq), lanes]
            if precise:
                w_hi, w_lo = _split(w)
                pv = _dot(w_hi, v) + (_dot(w_hi, vl_ref[pl.ds(ks, tq), lanes]) + _dot(w_lo, v))
            else:
                pv = _dot(w.astype(BF16), v)
            acc_ref[h] += pv
            carry_ref[h] = carry - jnp.broadcast_to(jnp.sum(sp, axis=1, keepdims=True), carry.shape)

    def sweep(precise):
        block(qi, True, precise)

        def body(j, _):
            block(qi - 1 - j, False, precise)
            return 0

        lax.fori_loop(0, qi, body, 0)

    @pl.when(qi < nq - 1)
    def _():
        sweep(False)

    @pl.when(qi == nq - 1)
    def _():
        sweep(True)

    for p in range(SB_HEADS // 2):
        o_ref[:, p * LANES:(p + 1) * LANES] = jnp.where(low_half, acc_ref[2 * p], acc_ref[2 * p + 1])


def _sb_prompt(bias, q_hi, q_lo, kt_hi, kt_lo, v_hi, v_lo, *, batch, seq):
    n, d_sb = q_hi.shape
    nq = seq // TQ
    qspec = pl.BlockSpec((TQ, d_sb), lambda b, i: (b * nq + i, 0))
    kspec = pl.BlockSpec((1, d_sb, seq), lambda b, i: (b, 0, 0))
    vspec = pl.BlockSpec((seq, d_sb), lambda b, i: (b, 0))
    return pl.pallas_call(
        _sb_prompt_kernel,
        grid=(batch, nq),
        in_specs=[pl.BlockSpec(memory_space=pltpu.SMEM), qspec, qspec, kspec, kspec, vspec, vspec],
        out_specs=qspec,
        out_shape=jax.ShapeDtypeStruct((n, d_sb), F32),
        scratch_shapes=[pltpu.VMEM((SB_HEADS, TQ, LANES), F32), pltpu.VMEM((SB_HEADS, TQ, LANES), F32)],
        compiler_params=_cparams(2),
        name="sb_prompt",
    )(bias, q_hi, q_lo, kt_hi, kt_lo, v_hi, v_lo)


def _sb_sample_kernel(pt_ref, q_ref, bias_ref, *refs, pp):
    del pt_ref
    k_refs, v_refs = refs[:pp], refs[pp:2 * pp]
    o_ref, carry_ref, acc_ref = refs[2 * pp:]
    j = pl.program_id(1)
    d_sb = SB_HEADS * SB_HEAD_DIM
    page = k_refs[0].shape[-1]

    @pl.when(j == 0)
    def _():
        carry_ref[...] = jnp.zeros_like(carry_ref)
        acc_ref[...] = jnp.zeros_like(acc_ref)

    head_of_lane = lax.broadcasted_iota(I32, (SB_HEADS, d_sb), 1) // SB_HEAD_DIM
    row = lax.broadcasted_iota(I32, (SB_HEADS, d_sb), 0)
    own = head_of_lane == row
    q_rows = jnp.broadcast_to(q_ref[0], (SB_HEADS, d_sb))
    q_bd = jnp.where(own, q_rows, 0.0)
    r = lax.broadcasted_iota(I32, (page, page), 0)
    c = lax.broadcasted_iota(I32, (page, page), 1)
    tri = jnp.where(r >= c, 1.0, 0.0).astype(BF16)

    def stacked_split(a):
        hi = a.astype(BF16).astype(F32)
        return jnp.concatenate([hi, a - hi], axis=0).astype(BF16)

    def stack3(a, b, nt=False):
        b_hi, b_lo = _split(b)
        dot = _dot_nt if nt else _dot
        m = a.shape[0]
        both = dot(stacked_split(a), b_hi)
        return both[:m] + (both[m:] + dot(a.astype(BF16), b_lo))

    order = list(reversed(range(pp)))
    zs = [stack3(q_bd, k_refs[pg][0].reshape(d_sb, page)) + bias_ref[...] for pg in order]
    sps = [_softplus(z) for z in zs]
    totals = [jnp.broadcast_to(jnp.sum(sp, axis=1, keepdims=True), carry_ref.shape) for sp in sps]
    cums = []
    for sp in sps:
        both = _dot(stacked_split(sp), tri)
        cums.append(both[:SB_HEADS] + both[SB_HEADS:])
    carry = carry_ref[...]
    acc = acc_ref[...]
    for n, pg in enumerate(order):
        w = jnp.exp(zs[n] + carry - cums[n])
        vt = v_refs[pg][0].reshape(d_sb, page)
        acc = acc + stack3(w, vt, nt=True)
        carry = carry - totals[n]
    carry_ref[...] = carry
    acc_ref[...] = acc

    @pl.when(j == pl.num_programs(1) - 1)
    def _():
        o_ref[0] = jnp.sum(jnp.where(own, acc, 0.0), axis=0, keepdims=True)


def _sb_sample(page_table, q3, bias_b, kt_pages, vt_pages):
    db, n_pages = page_table.shape
    d_sb = q3.shape[-1]
    page = kt_pages.shape[-1]
    pp = PAGES_PER_STEP
    assert n_pages % pp == 0
    nchunk = n_pages // pp

    def page_spec(pg):
        return pl.BlockSpec((1, SB_HEADS, SB_HEAD_DIM, page),
                            lambda b, j, pt: (pt[b, (nchunk - 1 - j) * pp + pg], 0, 0, 0))

    grid_spec = pltpu.PrefetchScalarGridSpec(
        num_scalar_prefetch=1,
        grid=(db, nchunk),
        in_specs=[pl.BlockSpec((1, 1, d_sb), lambda b, j, pt: (b, 0, 0)),
                  pl.BlockSpec((SB_HEADS, page), lambda b, j, pt: (0, 0))]
        + [page_spec(pg) for pg in range(pp)] * 2,
        out_specs=pl.BlockSpec((1, 1, d_sb), lambda b, j, pt: (b, 0, 0)),
        scratch_shapes=[pltpu.VMEM((SB_HEADS, page), F32), pltpu.VMEM((SB_HEADS, d_sb), F32)],
    )
    return pl.pallas_call(
        functools.partial(_sb_sample_kernel, pp=pp),
        grid_spec=grid_spec,
        out_shape=jax.ShapeDtypeStruct((db, 1, d_sb), F32),
        compiler_params=_cparams(2),
        name="sb_sample",
    )(page_table, q3, bias_b, *([kt_pages] * pp), *([vt_pages] * pp))


def _route(hn2, wr_hi_ref, wr_lo_ref, rb_ref, dense):
    m = hn2.shape[0]
    x_hi, x_lo = _split(hn2)
    w_hi, w_lo = wr_hi_ref[...], wr_lo_ref[...]
    lg = (_dot_nt(w_hi, x_hi) + _dot_nt(w_lo, x_hi)) + (_dot_nt(w_hi, x_lo) + _dot_nt(w_lo, x_lo)) + rb_ref[...]

    g = [lg[k:k + 1, :] for k in range(N_EXPERT_GROUPS)]
    gmax = functools.reduce(jnp.maximum, g)
    gidx = jnp.full((1, m), N_EXPERT_GROUPS - 1, I32)
    for k in reversed(range(N_EXPERT_GROUPS - 1)):
        gidx = jnp.where(g[k] == gmax, k, gidx)
    p_group = 1.0 / functools.reduce(lambda a, b: a + b, [jnp.exp(gk - gmax) for gk in g])

    e = []
    for k in range(EXPERTS_PER_GROUP):
        last = 8 + (N_EXPERT_GROUPS - 1) * EXPERTS_PER_GROUP + k
        ek = lg[last:last + 1, :]
        for gi in reversed(range(N_EXPERT_GROUPS - 1)):
            row = 8 + gi * EXPERTS_PER_GROUP + k
            ek = jnp.where(gidx == gi, lg[row:row + 1, :], ek)
        e.append(ek)
    v1 = functools.reduce(jnp.maximum, e)
    i1 = jnp.full((1, m), EXPERTS_PER_GROUP - 1, I32)
    for k in reversed(range(EXPERTS_PER_GROUP - 1)):
        i1 = jnp.where(e[k] == v1, k, i1)
    e2 = [jnp.where(i1 == k, -jnp.inf, e[k]) for k in range(EXPERTS_PER_GROUP)]
    v2 = functools.reduce(jnp.maximum, e2)
    i2 = jnp.full((1, m), EXPERTS_PER_GROUP - 1, I32)
    for k in reversed(range(EXPERTS_PER_GROUP - 1)):
        i2 = jnp.where(e2[k] == v2, k, i2)
    t = jnp.exp(v2 - v1)
    w1 = p_group / (1.0 + t)
    w2 = p_group * t / (1.0 + t)
    rows = lax.broadcasted_iota(I32, (LANES, m), 0)

    if dense:
        e1 = gidx * EXPERTS_PER_GROUP + i1
        e2id = gidx * EXPERTS_PER_GROUP + i2
        coef_t = jnp.where(rows == e1, w1, 0.0) + jnp.where(rows == e2id, w2, 0.0)
        return None, coef_t.T

    first_low = i1 < i2
    ia = jnp.where(first_low, i1, i2)
    ib = jnp.where(first_low, i2, i1)
    ca = jnp.where(first_low, w1, w2)
    cb = jnp.where(first_low, w2, w1)
    pidx = jnp.zeros((1, m), I32)
    for n, (pa, pb) in enumerate(PAIRS):
        pidx = jnp.where(jnp.logical_and(ia == pa, ib == pb), n, pidx)
    bucket = gidx * len(PAIRS) + pidx
    coef_t = jnp.where(rows == 0, ca, jnp.where(rows == 1, cb, 0.0))
    return bucket, coef_t.T


def _ffn_norm_route_store(h1, gf_ref, wr_hi_ref, wr_lo_ref, rb_ref, h1_ref, xe_ref, bk_ref):
    d = h1.shape[1]
    hn2 = _rms(h1, gf_ref[...])
    bucket, coef = _route(hn2, wr_hi_ref, wr_lo_ref, rb_ref, dense=bk_ref is None)
    h1_ref[...] = h1
    xe_ref[:, :d] = hn2
    xe_ref[:, d:] = coef
    if bk_ref is not None:
        bk_ref[...] = bucket


def _pool_windows(xp_ref, inv_cnt):
    m = xp_ref.shape[0] - HALO_POOL
    gd = xp_ref.shape[1] // len(POOL_WINDOWS)
    outs = []
    for gi, w in enumerate(POOL_WINDOWS):
        lanes = slice(gi * gd, (gi + 1) * gd)
        s = xp_ref[HALO_POOL:HALO_POOL + m, lanes]
        cur = s
        for j in range(1, w):
            s = s + xp_ref[HALO_POOL - j:HALO_POOL - j + m, lanes]
        outs.append(s * inv_cnt(w) - cur)
    return outs


def _mix_ab_kernel(hp_ref, osbp_ref, up_ref, uprev_ref, hs_ref, osbs_ref, us_ref, hist_ref,
                   wpool_ref, wpool_lo_ref, pscale_ref, wout_ref, wout_lo_ref, gf_ref, wr_hi_ref, wr_lo_ref, rb_ref,
                   h1p_ref, xep_ref, bkp_ref, h1s_ref, xes_ref, xp_ref, *, npt, tps, pos0):
    i = pl.program_id(0)
    d_sb = osbp_ref.shape[1]

    def tail(pooled, osb, h):
        o_pool = jnp.concatenate(
            [_dot3(*_split(pooled[gi]), wpool_ref[gi], wpool_lo_ref[gi]) for gi in range(len(POOL_WINDOWS))],
            axis=1) * pscale_ref[...]
        mix = (_mm(osb, wout_ref, wout_lo_ref, True, rows=slice(0, d_sb))
               + _mm(o_pool, wout_ref, wout_lo_ref, True, rows=slice(d_sb, None)))
        return h + mix

    @pl.when(i < npt)
    def _():
        tin = i % tps
        xp_ref[:HALO_POOL, :] = jnp.where(tin == 0, 0.0, uprev_ref[...])
        xp_ref[HALO_POOL:, :] = up_ref[...]
        pos = tin * TM + lax.broadcasted_iota(I32, (TM, 1), 0)

        def inv_cnt(w):
            return 1.0 / jnp.minimum(pos + 1, w).astype(F32)

        h1 = tail(_pool_windows(xp_ref, inv_cnt), osbp_ref[...], hp_ref[...])
        _ffn_norm_route_store(h1, gf_ref, wr_hi_ref, wr_lo_ref, rb_ref, h1p_ref, xep_ref, bkp_ref)

    @pl.when(i == npt)
    def _():
        gd = us_ref.shape[1] // len(POOL_WINDOWS)
        db = hist_ref.shape[1]
        u = us_ref[...]
        pooled = []
        for gi, w in enumerate(POOL_WINDOWS):
            lanes = slice(gi * gd, (gi + 1) * gd)
            s = u[:db, lanes]
            for j in range(1, w):
                s = s + hist_ref[POOL_HIST - j][:, lanes]
            s = jnp.concatenate([s, u[db:, lanes]], axis=0)
            pooled.append(s * (1.0 / min(pos0 + 1, w)) - u[:, lanes])
        h1 = tail(pooled, osbs_ref[...], hs_ref[...])
        _ffn_norm_route_store(h1, gf_ref, wr_hi_ref, wr_lo_ref, rb_ref, h1s_ref, xes_ref, None)


def _mix_ab(h_p, osb_p, u_p, h_s, osb_s, u_s, hist, wpool, wpool_lo, pscale, wout, wout_lo, gf, wr_hi, wr_lo, rb, *, seq, pos0):
    n, d = h_p.shape
    npt = n // TM
    tps = seq // TM
    d_sb = osb_p.shape[1]
    d_pool = u_p.shape[1]
    hb = TM // HALO_POOL
    consts = (wpool, wpool_lo, pscale, wout, wout_lo, gf, wr_hi, wr_lo, rb)
    in_specs = [
        _pspec(d, npt), _pspec(d_sb, npt), _pspec(d_pool, npt),
        pl.BlockSpec((HALO_POOL, d_pool), lambda i: (jnp.maximum(jnp.minimum(i, npt - 1) * hb - 1, 0), 0)),
        _const_spec((SP, d)), _const_spec((SP, d_sb)), _const_spec((SP, d_pool)), _const_spec(hist.shape),
    ] + [_const_spec(c.shape) for c in consts]
    de = d + LANES
    out_shape = (
        jax.ShapeDtypeStruct((n, d), F32), jax.ShapeDtypeStruct((n, de), F32), jax.ShapeDtypeStruct((1, n), I32),
        jax.ShapeDtypeStruct((SP, d), F32), jax.ShapeDtypeStruct((SP, de), F32),
    )
    out_specs = (
        _pspec(d, npt), _pspec(de, npt), pl.BlockSpec((1, TM), lambda i: (0, jnp.minimum(i, npt - 1))),
        _const_spec((SP, d)), _const_spec((SP, de)),
    )
    return pl.pallas_call(
        functools.partial(_mix_ab_kernel, npt=npt, tps=tps, pos0=pos0),
        grid=(npt + 1,),
        in_specs=in_specs, out_specs=out_specs, out_shape=out_shape,
        scratch_shapes=[pltpu.VMEM((HALO_POOL + TM, d_pool), F32)],
        compiler_params=_cparams(),
        name="mix_ab",
    )(h_p, osb_p, u_p, u_p, h_s, osb_s, u_s, hist, *consts)


def _sort_kernel(b_ref, pos_ref, tiles_ref, carry_ref, start_ref):
    ps = pl.program_id(0)
    j = pl.program_id(1)
    cw = b_ref.shape[1]
    rows = lax.broadcasted_iota(I32, (NB_ROWS, cw), 0)
    onehot = rows == b_ref[...]
    oh = jnp.where(onehot, 1.0, 0.0)
    counts = jnp.broadcast_to(jnp.sum(oh, axis=1, keepdims=True), (NB_ROWS, LANES))

    @pl.when(jnp.logical_and(ps == 0, j == 0))
    def _():
        carry_ref[...] = jnp.zeros_like(carry_ref)

    @pl.when(ps == 0)
    def _():
        carry_ref[...] += counts

    @pl.when(jnp.logical_and(ps == 1, j == 0))
    def _():
        real = lax.broadcasted_iota(I32, (NB_ROWS, LANES), 0) < N_BUCKETS
        tiles = jnp.where(real, jnp.floor((carry_ref[...] + (TMX - 1)) * (1.0 / TMX)), 0.0)
        rr = lax.broadcasted_iota(I32, (NB_ROWS, NB_ROWS), 0)
        cc = lax.broadcasted_iota(I32, (NB_ROWS, NB_ROWS), 1)
        below = jnp.where(cc < rr, 1.0, 0.0).astype(BF16)
        start_ref[...] = _dot(below, tiles.astype(BF16)) * TMX
        tiles_ref[...] = tiles
        carry_ref[...] = jnp.zeros_like(carry_ref)

    @pl.when(ps == 1)
    def _():
        rj = lax.broadcasted_iota(I32, (cw, cw), 0)
        ct = lax.broadcasted_iota(I32, (cw, cw), 1)
        before = jnp.where(rj < ct, 1.0, 0.0).astype(BF16)
        rank = _dot(oh.astype(BF16), before)
        val = rank + carry_ref[:, 0:1] + start_ref[:, 0:1]
        pos_ref[...] = jnp.sum(jnp.where(onehot, val, 0.0), axis=0, keepdims=True).astype(I32)
        carry_ref[...] += counts


def _sort_positions(buckets):
    npad = buckets.shape[1]
    nch = npad // SORT_CHUNK
    return pl.pallas_call(
        _sort_kernel,
        grid=(2, nch),
        in_specs=[pl.BlockSpec((1, SORT_CHUNK), lambda ps, j: (0, j))],
        out_specs=(pl.BlockSpec((1, SORT_CHUNK), lambda ps, j: (0, j * ps)),
                   pl.BlockSpec((NB_ROWS, LANES), lambda ps, j: (0, 0))),
        out_shape=(jax.ShapeDtypeStruct((1, npad), I32), jax.ShapeDtypeStruct((NB_ROWS, LANES), F32)),
        scratch_shapes=[pltpu.VMEM((NB_ROWS, LANES), F32), pltpu.VMEM((NB_ROWS, LANES), F32)],
        compiler_params=_cparams(2),
        name="moe_sort",
    )(buckets)


def _move_rows_kernel(pos_ref, *refs, scatter):
    if scatter:
        tok_ref, _, sorted_ref, sems = refs
    else:
        sorted_ref, tok_ref, sems = refs
    nb = tok_ref.shape[0] // MOVE_BATCH

    def copy(tok_rows, sorted_rows, slot):
        src, dst = (tok_ref.at[tok_rows], sorted_ref.at[sorted_rows])
        if not scatter:
            src, dst = dst, src
        return pltpu.make_async_copy(src, dst, sems.at[slot])

    def batch_wait(slot):
        copy(pl.ds(0, MOVE_BATCH), pl.ds(0, MOVE_BATCH), slot).wait()

    def batch(b, _):
        slot = lax.rem(b, MOVE_SEMS)

        @pl.when(b >= MOVE_SEMS)
        def _():
            batch_wait(slot)

        def row(r, _):
            t = b * MOVE_BATCH + r
            copy(pl.ds(t, 1), pl.ds(pos_ref[0, t], 1), slot).start()
            return 0

        lax.fori_loop(0, MOVE_BATCH, row, 0, unroll=8)
        return 0

    lax.fori_loop(0, nb, batch, 0)

    def drain(b, _):
        batch_wait(lax.rem(b, MOVE_SEMS))
        return 0

    lax.fori_loop(max(nb - MOVE_SEMS, 0), nb, drain, 0)


def _scatter_rows(pos, src, dst):
    n, cols = src.shape
    chunk = min(MOVE_CHUNK, n)
    assert n % chunk == 0 and chunk % MOVE_BATCH == 0
    return pl.pallas_call(
        functools.partial(_move_rows_kernel, scatter=True),
        grid=(n // chunk,),
        in_specs=[pl.BlockSpec((1, chunk), lambda i: (0, i), memory_space=pltpu.SMEM),
                  pl.BlockSpec((chunk, cols), lambda i: (i, 0)),
                  pl.BlockSpec(memory_space=pl.ANY)],
        out_specs=pl.BlockSpec(memory_space=pl.ANY),
        out_shape=jax.ShapeDtypeStruct(dst.shape, dst.dtype),
        scratch_shapes=[pltpu.SemaphoreType.DMA((MOVE_SEMS,))],
        input_output_aliases={2: 0},
        compiler_params=_cparams(),
        name="moe_scatter",
    )(pos, src, dst)


def _gather_rows(pos, src):
    n = pos.shape[1]
    cols = src.shape[1]
    chunk = min(MOVE_CHUNK, n)
    assert n % chunk == 0 and chunk % MOVE_BATCH == 0
    return pl.pallas_call(
        functools.partial(_move_rows_kernel, scatter=False),
        grid=(n // chunk,),
        in_specs=[pl.BlockSpec((1, chunk), lambda i: (0, i), memory_space=pltpu.SMEM),
                  pl.BlockSpec(memory_space=pl.ANY)],
        out_specs=pl.BlockSpec((chunk, cols), lambda i: (i, 0)),
        out_shape=jax.ShapeDtypeStruct((n, cols), src.dtype),
        scratch_shapes=[pltpu.SemaphoreType.DMA((MOVE_SEMS,))],
        compiler_params=_cparams(),
        name="moe_gather",
    )(pos, src)


def _experts_kernel(ea_ref, eb_ref, valid_ref, xs_ref, wga_ref, wua_ref, wda_ref, wgb_ref, wub_ref, wdb_ref, ys_ref):
    i = pl.program_id(0)
    d = ys_ref.shape[1]

    @pl.when(valid_ref[i] == 1)
    def _():
        x = xs_ref[:, :d].astype(BF16)

        def expert(wg_ref, wu_ref, wd_ref):
            g = _dot(x, wg_ref[0].astype(BF16))
            u = _dot(x, wu_ref[0].astype(BF16))
            hid = g * _sigmoid(g) * u
            return _dot(hid.astype(BF16), wd_ref[0].astype(BF16))

        ya = expert(wga_ref, wua_ref, wda_ref)
        yb = expert(wgb_ref, wub_ref, wdb_ref)
        ys_ref[...] = xs_ref[:, d:d + 1] * ya + xs_ref[:, d + 1:d + 2] * yb

    @pl.when(valid_ref[i] == 0)
    def _():
        ys_ref[...] = jnp.zeros_like(ys_ref)


def _experts(ea, eb, valid, xs, w_gate, w_up, w_down):
    ns, de = xs.shape
    d = de - LANES
    dx = w_gate.shape[-1]
    ntiles = ns // TMX

    def wspec(shape, which):
        return pl.BlockSpec((1,) + shape, lambda i, ea, eb, valid: ((ea, eb)[which][i], 0, 0))

    grid_spec = pltpu.PrefetchScalarGridSpec(
        num_scalar_prefetch=3,
        grid=(ntiles,),
        in_specs=[pl.BlockSpec((TMX, de), lambda i, ea, eb, valid: (i, 0)),
                  wspec((d, dx), 0), wspec((d, dx), 0), wspec((dx, d), 0),
                  wspec((d, dx), 1), wspec((d, dx), 1), wspec((dx, d), 1)],
        out_specs=pl.BlockSpec((TMX, d), lambda i, ea, eb, valid: (i, 0)),
    )
    return pl.pallas_call(
        _experts_kernel,
        grid_spec=grid_spec,
        out_shape=jax.ShapeDtypeStruct((ns, d), F32),
        compiler_params=_cparams(),
        name="moe_experts",
    )(ea, eb, valid, xs, w_gate, w_up, w_down, w_gate, w_up, w_down)


def _moe_sparse(xe, bk, w_gate, w_up, w_down, layer):
    n_tok, de = xe.shape
    npad = -(-n_tok // SORT_CHUNK) * SORT_CHUNK
    buckets = jnp.concatenate([bk, jnp.full((1, npad - n_tok), N_BUCKETS, I32)], axis=1)
    pos, tiles = _sort_positions(buckets)

    ntiles = -(-n_tok // TMX) + N_BUCKETS
    tiles_per_bucket = tiles[:N_BUCKETS, 0].astype(I32)
    cum = jnp.cumsum(tiles_per_bucket)
    tile_ids = jnp.arange(ntiles, dtype=I32)
    valid = tile_ids < cum[-1]
    tb = jnp.sum((jnp.minimum(tile_ids, cum[-1] - 1)[:, None] >= cum[None, :]).astype(I32), axis=1)
    group, pidx = tb // len(PAIRS), tb % len(PAIRS)
    pa = jnp.asarray([p[0] for p in PAIRS], I32)[pidx]
    pb = jnp.asarray([p[1] for p in PAIRS], I32)[pidx]
    first = layer * N_EXPERTS + group * EXPERTS_PER_GROUP
    flat = lambda w: w.reshape((-1,) + w.shape[2:])

    pos = pos[:, :n_tok]
    xs = _scatter_rows(pos, xe, jnp.zeros((ntiles * TMX, de), F32))
    ys = _experts(first + pa, first + pb, valid.astype(I32), xs, flat(w_gate), flat(w_up), flat(w_down))
    return _gather_rows(pos, ys)


def _experts_dense_kernel(xe_ref, wg_ref, wu_ref, wd_ref, y_ref):
    e = pl.program_id(0)
    d = y_ref.shape[1]

    @pl.when(e == 0)
    def _():
        y_ref[...] = jnp.zeros_like(y_ref)

    x_hi, x_lo = _split(xe_ref[:, :d])
    g = _dot3(x_hi, x_lo, *_split(wg_ref[0, 0]))
    u = _dot3(x_hi, x_lo, *_split(wu_ref[0, 0]))
    hid = g * _sigmoid(g) * u
    y = _dot3(*_split(hid), *_split(wd_ref[0, 0]))
    lane = lax.broadcasted_iota(I32, (xe_ref.shape[0], LANES), 1)
    coef = jnp.sum(jnp.where(lane == e, xe_ref[:, d:], 0.0), axis=1, keepdims=True)
    y_ref[...] += coef * y


def _moe_dense(xe, w_gate, w_up, w_down, layer):
    m, de = xe.shape
    d = de - LANES
    _, n_exp, _, dx = w_gate.shape
    return pl.pallas_call(
        _experts_dense_kernel,
        grid=(n_exp,),
        in_specs=[_const_spec((m, de)),
                  pl.BlockSpec((1, 1, d, dx), lambda e: (layer, e, 0, 0)), pl.BlockSpec((1, 1, d, dx), lambda e: (layer, e, 0, 0)),
                  pl.BlockSpec((1, 1, dx, d), lambda e: (layer, e, 0, 0))],
        out_specs=_const_spec((m, d)),
        out_shape=jax.ShapeDtypeStruct((m, d), F32),
        compiler_params=_cparams(),
        name="moe_dense",
    )(xe, w_gate, w_up, w_down)


def _ple(h, y, p, gpg_ref, wpg_ref, wpg_lo_ref, bpg_ref, wple_ref, wple_lo_ref, precise):
    h2 = h + y
    gate = _sigmoid(_mm(_rms(h2, gpg_ref[...]), wpg_ref, wpg_lo_ref, precise) + bpg_ref[...])
    return h2 + gate * _mm(p, wple_ref, wple_lo_ref, precise)


def _ple_inproj_c_kernel(hp_ref, yp_ref, pp_ref, hs_ref, ys_ref, ps_ref,
                         gpg_ref, wpg_ref, wpg_lo_ref, bpg_ref, wple_ref, wple_lo_ref, gm_ref, win_ref, win_lo_ref,
                         h3p_ref, ucp_ref, h3s_ref, ucs_ref, *, npt):
    i = pl.program_id(0)

    def run(h_ref, y_ref, p_ref, h3_ref, uc_ref, precise):
        h3 = _ple(h_ref[...], y_ref[...], p_ref[...], gpg_ref, wpg_ref, wpg_lo_ref, bpg_ref, wple_ref, wple_lo_ref, precise)
        h3_ref[...] = h3
        ag = _mm(_rms(h3, gm_ref[...]), win_ref, win_lo_ref, precise)
        dc = ag.shape[1] // 2
        uc_ref[...] = ag[:, :dc] * _sigmoid(ag[:, dc:])

    @pl.when(i < npt)
    def _():
        run(hp_ref, yp_ref, pp_ref, h3p_ref, ucp_ref, False)

    @pl.when(i == npt)
    def _():
        run(hs_ref, ys_ref, ps_ref, h3s_ref, ucs_ref, True)


def _ple_inproj_c(h_p, y_p, p_all, layer, h_s, y_s, p_s, gpg, wpg, wpg_lo, bpg, wple, wple_lo, gm, win, win_lo):
    n, d = h_p.shape
    npt = n // TM
    dple = p_all.shape[1]
    dc = win.shape[1] // 2
    consts = (gpg, wpg, wpg_lo, bpg, wple, wple_lo, gm, win, win_lo)
    return pl.pallas_call(
        functools.partial(_ple_inproj_c_kernel, npt=npt),
        grid=(npt + 1,),
        in_specs=[_pspec(d, npt), _pspec(d, npt), _pspec(dple, npt, first_block=layer * npt),
                  _const_spec((SP, d)), _const_spec((SP, d)), _const_spec((SP, dple))]
        + [_const_spec(c.shape) for c in consts],
        out_specs=(_pspec(d, npt), _pspec(dc, npt), _const_spec((SP, d)), _const_spec((SP, dc))),
        out_shape=(jax.ShapeDtypeStruct((n, d), F32), jax.ShapeDtypeStruct((n, dc), F32),
                   jax.ShapeDtypeStruct((SP, d), F32), jax.ShapeDtypeStruct((SP, dc), F32)),
        compiler_params=_cparams(),
        name="ple_inproj_c",
    )(h_p, y_p, p_all, h_s, y_s, p_s, *consts)


def _ple_final_kernel(hp_ref, yp_ref, pp_ref, hs_ref, ys_ref, ps_ref,
                      gpg_ref, wpg_ref, wpg_lo_ref, bpg_ref, wple_ref, wple_lo_ref, gfin_ref, op_ref, os_ref, *, npt):
    i = pl.program_id(0)

    def run(h_ref, y_ref, p_ref, o_ref, precise):
        h = _ple(h_ref[...], y_ref[...], p_ref[...], gpg_ref, wpg_ref, wpg_lo_ref, bpg_ref, wple_ref, wple_lo_ref, precise)
        o_ref[...] = _rms(h, gfin_ref[...])

    @pl.when(i < npt)
    def _():
        run(hp_ref, yp_ref, pp_ref, op_ref, False)

    @pl.when(i == npt)
    def _():
        run(hs_ref, ys_ref, ps_ref, os_ref, True)


def _ple_final(h_p, y_p, p_all, layer, h_s, y_s, p_s, gpg, wpg, wpg_lo, bpg, wple, wple_lo, gfin):
    n, d = h_p.shape
    npt = n // TM
    dple = p_all.shape[1]
    consts = (gpg, wpg, wpg_lo, bpg, wple, wple_lo, gfin)
    return pl.pallas_call(
        functools.partial(_ple_final_kernel, npt=npt),
        grid=(npt + 1,),
        in_specs=[_pspec(d, npt), _pspec(d, npt), _pspec(dple, npt, first_block=layer * npt),
                  _const_spec((SP, d)), _const_spec((SP, d)), _const_spec((SP, dple))]
        + [_const_spec(c.shape) for c in consts],
        out_specs=(_pspec(d, npt), _const_spec((SP, d))),
        out_shape=(jax.ShapeDtypeStruct((n, d), F32), jax.ShapeDtypeStruct((SP, d), F32)),
        compiler_params=_cparams(),
        name="ple_final",
    )(h_p, y_p, p_all, h_s, y_s, p_s, *consts)


def _conv_tail(y, h, bdw_ref, lng_ref, lnb_ref, wout_ref, wout_lo_ref, precise):
    y = y + bdw_ref[...]
    mu = jnp.mean(y, axis=-1, keepdims=True)
    yc = y - mu
    yn = yc * lax.rsqrt(jnp.mean(yc * yc, axis=-1, keepdims=True) + EPS) * lng_ref[...] + lnb_ref[...]
    act = yn * _sigmoid(yn)
    return h + _mm(act, wout_ref, wout_lo_ref, precise)


def _mix_c_kernel(hp_ref, ucp_ref, ucprev_ref, hs_ref, ucs_ref, hist_ref,
                  wdw_ref, bdw_ref, lng_ref, lnb_ref, wout_ref, wout_lo_ref, gf_ref, wr_hi_ref, wr_lo_ref, rb_ref,
                  h1p_ref, xep_ref, bkp_ref, h1s_ref, xes_ref, xp_ref, *, npt, tps):
    i = pl.program_id(0)

    @pl.when(i < npt)
    def _():
        tin = i % tps
        xp_ref[:HALO_CONV, :] = jnp.where(tin == 0, 0.0, ucprev_ref[...])
        xp_ref[HALO_CONV:, :] = ucp_ref[...]
        off = HALO_CONV - CONV_HIST
        y = wdw_ref[0:1, :] * xp_ref[off:off + TM, :]
        for j in range(1, CONV_WIDTH):
            y = y + wdw_ref[j:j + 1, :] * xp_ref[off + j:off + j + TM, :]
        h1 = _conv_tail(y, hp_ref[...], bdw_ref, lng_ref, lnb_ref, wout_ref, wout_lo_ref, False)
        _ffn_norm_route_store(h1, gf_ref, wr_hi_ref, wr_lo_ref, rb_ref, h1p_ref, xep_ref, bkp_ref)

    @pl.when(i == npt)
    def _():
        db = hist_ref.shape[1]
        y = wdw_ref[CONV_WIDTH - 1:CONV_WIDTH, :] * ucs_ref[...]
        y_hist = wdw_ref[0:1, :] * hist_ref[0]
        for j in range(1, CONV_HIST):
            y_hist = y_hist + wdw_ref[j:j + 1, :] * hist_ref[j]
        y = jnp.concatenate([y[:db] + y_hist, y[db:]], axis=0)
        h1 = _conv_tail(y, hs_ref[...], bdw_ref, lng_ref, lnb_ref, wout_ref, wout_lo_ref, True)
        _ffn_norm_route_store(h1, gf_ref, wr_hi_ref, wr_lo_ref, rb_ref, h1s_ref, xes_ref, None)


def _mix_c(h_p, uc_p, h_s, uc_s, hist, wdw, bdw, lng, lnb, wout, wout_lo, gf, wr_hi, wr_lo, rb, *, seq):
    n, d = h_p.shape
    npt = n // TM
    tps = seq // TM
    dc = uc_p.shape[1]
    hb = TM // HALO_CONV
    consts = (wdw, bdw, lng, lnb, wout, wout_lo, gf, wr_hi, wr_lo, rb)
    in_specs = [
        _pspec(d, npt), _pspec(dc, npt),
        pl.BlockSpec((HALO_CONV, dc), lambda i: (jnp.maximum(jnp.minimum(i, npt - 1) * hb - 1, 0), 0)),
        _const_spec((SP, d)), _const_spec((SP, dc)), _const_spec(hist.shape),
    ] + [_const_spec(c.shape) for c in consts]
    de = d + LANES
    out_shape = (
        jax.ShapeDtypeStruct((n, d), F32), jax.ShapeDtypeStruct((n, de), F32), jax.ShapeDtypeStruct((1, n), I32),
        jax.ShapeDtypeStruct((SP, d), F32), jax.ShapeDtypeStruct((SP, de), F32),
    )
    out_specs = (
        _pspec(d, npt), _pspec(de, npt), pl.BlockSpec((1, TM), lambda i: (0, jnp.minimum(i, npt - 1))),
        _const_spec((SP, d)), _const_spec((SP, de)),
    )
    return pl.pallas_call(
        functools.partial(_mix_c_kernel, npt=npt, tps=tps),
        grid=(npt + 1,),
        in_specs=in_specs, out_specs=out_specs, out_shape=out_shape,
        scratch_shapes=[pltpu.VMEM((HALO_CONV + TM, dc), F32)],
        compiler_params=_cparams(),
        name="mix_c",
    )(h_p, uc_p, uc_p, h_s, uc_s, hist, *consts)


def _router_weights(w_rg, b_rg, w_re, b_re):
    d = w_rg.shape[0]
    wt = jnp.zeros((ROUTE_ROWS, d), F32).at[:N_EXPERT_GROUPS].set(w_rg.T).at[8:8 + w_re.shape[1]].set(w_re.T)
    hi, lo = _hi_lo(wt)
    rb = jnp.zeros((ROUTE_ROWS, 1), F32).at[:N_EXPERT_GROUPS, 0].set(b_rg).at[8:8 + w_re.shape[1], 0].set(b_re)
    return hi, lo, rb


def _pad_rows(x, rows):
    return jnp.concatenate([x, jnp.zeros((rows - x.shape[0],) + x.shape[1:], x.dtype)], axis=0)


def kernel(x_prompt, x_sample, p_prompt, p_sample, cache_k, cache_v, state_pool, state_conv, page_table, norm_mix, norm_ffn, norm_final, w_in_ab, sb_bias, w_pool, pool_scale, w_out_ab, w_in_c, w_dw, b_dw, ln_g_c, ln_b_c, w_out_c, w_rg, b_rg, w_re, b_re, w_gate_e, w_up_e, w_down_e, w_ple, w_ple_gate, b_ple_gate, g_ple_gate):
    batch, seq, d = x_prompt.shape
    db, dec_seq, _ = x_sample.shape
    depth = norm_mix.shape[0]
    d_sb = SB_HEADS * SB_HEAD_DIM
    n_pages = page_table.shape[1]
    page = cache_k.shape[2]
    pos0 = n_pages * page
    assert dec_seq == 1 and depth == 2 and db <= SP and db % 8 == 0 and w_gate_e.shape[1] == N_EXPERTS
    assert seq % TM == 0 and seq % TQ == 0 and TMX & (TMX - 1) == 0
    n_p = batch * seq

    row = lambda v: v.reshape(1, -1)
    x_p = x_prompt.reshape(n_p, d)
    x_s = _pad_rows(x_sample.reshape(db, d), SP)
    p_all = p_prompt.reshape(depth * n_p, -1)
    ps = [_pad_rows(p_sample[i].reshape(db, -1), SP) for i in range(depth)]

    w_in, w_in_lo = _hi_lo(w_in_ab[0])
    w_kv_t, w_kv_t_lo = _hi_lo(w_in_ab[0][:, d_sb:3 * d_sb].T)
    qh_p, ql_p, vh_p, vl_p, u_p, kt_p, vt_p, kth_p, ktl_p, q_s, u_s, k_s, v_s = _inproj_ab(
        x_p, x_s, row(norm_mix[0]), w_in, w_in_lo, w_kv_t, w_kv_t_lo, batch=batch, seq=seq)
    osb_p = _sb_prompt(sb_bias[0], qh_p, ql_p, kth_p, ktl_p, vh_p, vl_p, batch=batch, seq=seq)
    kt_pages = jnp.transpose(cache_k[0], (0, 2, 3, 1))
    vt_pages = jnp.transpose(cache_v[0], (0, 2, 3, 1))
    bias_b = jnp.broadcast_to(sb_bias[0][:, None], (SB_HEADS, page))
    osb_s = _sb_sample(page_table, q_s[:db].reshape(db, 1, d_sb), bias_b, kt_pages, vt_pages)
    osb_s = _pad_rows(osb_s.reshape(db, d_sb), SP)

    pool_hist = jnp.transpose(state_pool[0], (1, 0, 2))
    wr_hi, wr_lo, rb = _router_weights(w_rg[0], b_rg[0], w_re[0], b_re[0])
    w_pool_hi, w_pool_lo = _hi_lo(w_pool[0])
    w_out_hi, w_out_lo = _hi_lo(w_out_ab[0])
    h1_p, xe_p, bk_p, h1_s, xe_s = _mix_ab(
        x_p, osb_p, u_p, x_s, osb_s, u_s, pool_hist, w_pool_hi, w_pool_lo, row(pool_scale[0]),
        w_out_hi, w_out_lo, row(norm_ffn[0]), wr_hi, wr_lo, rb, seq=seq, pos0=pos0)
    y_p = _moe_sparse(xe_p, bk_p, w_gate_e, w_up_e, w_down_e, 0)
    y_s = _moe_dense(xe_s, w_gate_e, w_up_e, w_down_e, 0)

    w_pg, w_pg_lo = _hi_lo(w_ple_gate[0])
    w_pl, w_pl_lo = _hi_lo(w_ple[0])
    w_ic, w_ic_lo = _hi_lo(w_in_c[0])
    h3_p, uc_p, h3_s, uc_s = _ple_inproj_c(
        h1_p, y_p, p_all, 0, h1_s, y_s, ps[0], row(g_ple_gate[0]), w_pg, w_pg_lo, row(b_ple_gate[0]),
        w_pl, w_pl_lo, row(norm_mix[1]), w_ic, w_ic_lo)
    conv_hist = jnp.transpose(state_conv[0], (1, 0, 2))
    wr_hi, wr_lo, rb = _router_weights(w_rg[1], b_rg[1], w_re[1], b_re[1])
    w_oc, w_oc_lo = _hi_lo(w_out_c[0])
    h4_p, xe_p, bk_p, h4_s, xe_s = _mix_c(
        h3_p, uc_p, h3_s, uc_s, conv_hist, w_dw[0], row(b_dw[0]), row(ln_g_c[0]), row(ln_b_c[0]),
        w_oc, w_oc_lo, row(norm_ffn[1]), wr_hi, wr_lo, rb, seq=seq)
    y_p = _moe_sparse(xe_p, bk_p, w_gate_e, w_up_e, w_down_e, 1)
    y_s = _moe_dense(xe_s, w_gate_e, w_up_e, w_down_e, 1)
    w_pg, w_pg_lo = _hi_lo(w_ple_gate[1])
    w_pl, w_pl_lo = _hi_lo(w_ple[1])
    out_p, out_s = _ple_final(
        h4_p, y_p, p_all, 1, h4_s, y_s, ps[1], row(g_ple_gate[1]), w_pg, w_pg_lo, row(b_ple_gate[1]),
        w_pl, w_pl_lo, row(norm_final))

    def heads_last(t):
        return jnp.transpose(t.reshape(batch, SB_HEADS, SB_HEAD_DIM, seq), (0, 3, 1, 2))[None]

    y_prompt = out_p.reshape(batch, seq, d)
    y_sample = out_s[:db].reshape(db, 1, d)
    k_prompt, v_prompt = heads_last(kt_p), heads_last(vt_p)
    pool_prompt = u_p.reshape(batch, seq, -1)[:, seq - POOL_HIST:][None]
    conv_prompt = uc_p.reshape(batch, seq, -1)[:, seq - CONV_HIST:][None]
    k_sample = k_s[:db].reshape(1, db, 1, SB_HEADS, SB_HEAD_DIM)
    v_sample = v_s[:db].reshape(1, db, 1, SB_HEADS, SB_HEAD_DIM)
    pool_sample = jnp.transpose(jnp.concatenate([pool_hist[1:], u_s[None, :db]], axis=0), (1, 0, 2))[None]
    conv_sample = jnp.transpose(jnp.concatenate([conv_hist[1:], uc_s[None, :db]], axis=0), (1, 0, 2))[None]
    return (y_prompt, y_sample, k_prompt, v_prompt, pool_prompt, conv_prompt, k_sample, v_sample, pool_sample, conv_sample)
```

```python
import functools

import jax
import jax.numpy as jnp
from jax import lax
from jax.experimental import pallas as pl
from jax.experimental.pallas import tpu as pltpu

F32 = jnp.float32
BF16 = jnp.bfloat16
I32 = jnp.int32

EPS = 1e-6
SB_HEADS = 8
SB_HEAD_DIM = 64
POOL_WINDOWS = (2, 4, 8, 16)
POOL_HIST = max(POOL_WINDOWS) - 1
CONV_WIDTH = 31
CONV_HIST = CONV_WIDTH - 1
N_EXPERT_GROUPS = 4
EXPERTS_PER_GROUP = 4
N_EXPERTS = N_EXPERT_GROUPS * EXPERTS_PER_GROUP
PAIRS = ((0, 1), (0, 2), (0, 3), (1, 3), (1, 2), (2, 3))
N_BUCKETS = N_EXPERT_GROUPS * len(PAIRS)

LANES = 128
TM = 512
SP = 128
TQ = 256
TMX = 256
SORT_CHUNK = 512
NB_ROWS = 32
ROUTE_ROWS = 32
HALO_POOL = 16
HALO_CONV = 32
CONV_SUBLANES = 8
CONV_ROWS = 32
PAGES_PER_STEP = 16
MOVE_CHUNK = 2048
MOVE_BATCH = 128
MOVE_SEMS = 8
NEG_BIG = -1e30
SPLIT_FACTOR = 65537.0
VMEM_LIMIT = 56 * 1024 * 1024


def _cparams(n_axes=1, vmem=VMEM_LIMIT):
    return pltpu.CompilerParams(dimension_semantics=("arbitrary",) * n_axes, vmem_limit_bytes=vmem)


def _rms(x, g):
    return x * lax.rsqrt(jnp.mean(x * x, axis=-1, keepdims=True) + EPS) * g


def _sigmoid(x):
    return 1.0 / (1.0 + jnp.exp(-x))


def _softplus(z):
    return jnp.maximum(z, 0.0) + jnp.log(1.0 + jnp.exp(-jnp.abs(z)))


def _dot(a, b):
    return jnp.dot(a, b, preferred_element_type=F32)


def _dot_nt(a, b):
    return lax.dot_general(a, b, (((1,), (1,)), ((), ())), preferred_element_type=F32)


def _split(a):
    c = a * SPLIT_FACTOR
    hi = c - (c - a)
    return hi.astype(BF16), (a - hi).astype(BF16)


def _dot3(a_hi, a_lo, w_hi, w_lo):
    return _dot(a_hi, w_hi) + (_dot(a_hi, w_lo) + _dot(a_lo, w_hi))


def _mm(a, w_ref, wlo_ref, precise, rows=slice(None), cols=slice(None)):
    if precise:
        return _dot3(*_split(a), w_ref[rows, cols], wlo_ref[rows, cols])
    return _dot(a.astype(BF16), w_ref[rows, cols])


def _hi_lo(w):
    return _split(w)


def _pspec(cols, npt, rows=TM, first_block=0):
    return pl.BlockSpec((rows, cols), lambda i: (first_block + jnp.minimum(i, npt - 1), 0))


def _const_spec(shape):
    nd = len(shape)
    return pl.BlockSpec(shape, lambda i: (0,) * nd)


def _inproj_ab_kernel(xp_ref, xs_ref, g_ref, w_ref, wlo_ref, wt_ref, wtlo_ref,
                      qhp_ref, qlp_ref, vhp_ref, vlp_ref, up_ref, ktp_ref, vtp_ref, khtp_ref, kltp_ref,
                      qs_ref, us_ref, ks_ref, vs_ref, *, npt, tps, d_sb):
    i = pl.program_id(0)
    ucols = slice(3 * d_sb, None)
    last_in_seq = i % tps == tps - 1

    def prompt_tile(precise_q):
        hn, hn_lo = _split(_rms(xp_ref[...], g_ref[...]))

        def proj(cols):
            return _dot3(hn, hn_lo, w_ref[:, cols], wlo_ref[:, cols])

        qcols = slice(0, d_sb)
        q = (proj(qcols) if precise_q else _dot(hn, w_ref[:, qcols])) * (SB_HEAD_DIM ** -0.5)
        qhp_ref[...], qlp_ref[...] = _split(q)
        vu = proj(slice(2 * d_sb, None))
        vhp_ref[...], vlp_ref[...] = _split(vu[:, :d_sb])
        up_ref[...] = vu[:, d_sb:]
        kt = _dot_nt(wt_ref[:d_sb], hn) + (_dot_nt(wtlo_ref[...], hn) + _dot_nt(wt_ref[:d_sb], hn_lo))
        ktp_ref[0] = kt
        khtp_ref[0], kltp_ref[0] = _split(kt)
        vtp_ref[0] = _dot_nt(wt_ref[d_sb:], hn)

    @pl.when(jnp.logical_and(i < npt, last_in_seq))
    def _():
        prompt_tile(True)

    @pl.when(jnp.logical_and(i < npt, jnp.logical_not(last_in_seq)))
    def _():
        prompt_tile(False)

    @pl.when(i == npt)
    def _():
        pr = _mm(_rms(xs_ref[...], g_ref[...]), w_ref, wlo_ref, True)
        qs_ref[...] = pr[:, :d_sb] * (SB_HEAD_DIM ** -0.5)
        ks_ref[...] = pr[:, d_sb:2 * d_sb]
        vs_ref[...] = pr[:, 2 * d_sb:3 * d_sb]
        us_ref[...] = pr[:, ucols]


def _inproj_ab(x_p, x_s, g, w, w_lo, wt, wt_lo, *, batch, seq):
    n, d = x_p.shape
    npt = n // TM
    tps = seq // TM
    d_sb = SB_HEADS * SB_HEAD_DIM
    d_pool = w.shape[1] - 3 * d_sb
    tspec = pl.BlockSpec((1, d_sb, TM), lambda i: (jnp.minimum(i, npt - 1) // tps, 0, jnp.minimum(i, npt - 1) % tps))
    out_shape = (
        jax.ShapeDtypeStruct((n, d_sb), BF16),
        jax.ShapeDtypeStruct((n, d_sb), BF16),
        jax.ShapeDtypeStruct((n, d_sb), BF16),
        jax.ShapeDtypeStruct((n, d_sb), BF16),
        jax.ShapeDtypeStruct((n, d_pool), F32),
        jax.ShapeDtypeStruct((batch, d_sb, seq), F32),
        jax.ShapeDtypeStruct((batch, d_sb, seq), F32),
        jax.ShapeDtypeStruct((batch, d_sb, seq), BF16),
        jax.ShapeDtypeStruct((batch, d_sb, seq), BF16),
        jax.ShapeDtypeStruct((SP, d_sb), F32),
        jax.ShapeDtypeStruct((SP, d_pool), F32),
        jax.ShapeDtypeStruct((SP, d_sb), F32),
        jax.ShapeDtypeStruct((SP, d_sb), F32),
    )
    out_specs = (
        _pspec(d_sb, npt), _pspec(d_sb, npt), _pspec(d_sb, npt), _pspec(d_sb, npt), _pspec(d_pool, npt),
        tspec, tspec, tspec, tspec,
        _const_spec((SP, d_sb)), _const_spec((SP, d_pool)), _const_spec((SP, d_sb)), _const_spec((SP, d_sb)),
    )
    consts = (g, w, w_lo, wt, wt_lo)
    return pl.pallas_call(
        functools.partial(_inproj_ab_kernel, npt=npt, tps=tps, d_sb=d_sb),
        grid=(npt + 1,),
        in_specs=[_pspec(d, npt), _const_spec((SP, d))] + [_const_spec(c.shape) for c in consts],
        out_specs=out_specs,
        out_shape=out_shape,
        compiler_params=_cparams(),
        name="inproj_ab",
    )(x_p, x_s, *consts)


def _sb_prompt_kernel(bias_ref, qh_ref, ql_ref, kth_ref, ktl_ref, vh_ref, vl_ref, o_ref, carry_ref, acc_ref):
    qi = pl.program_id(1)
    nq = pl.num_programs(1)
    tq = qh_ref.shape[0]
    rk = lax.broadcasted_iota(I32, (tq, tq), 0)
    ck = lax.broadcasted_iota(I32, (tq, tq), 1)
    tri = jnp.where(rk >= ck, 1.0, 0.0).astype(BF16)

    carry_ref[...] = jnp.zeros_like(carry_ref)
    acc_ref[...] = jnp.zeros_like(acc_ref)

    low_half = lax.broadcasted_iota(I32, (tq, LANES), 1) < SB_HEAD_DIM
    causal = ck < rk

    def block(kj, diagonal, precise):
        ks = pl.multiple_of(kj * tq, tq)
        for h in range(SB_HEADS):
            p, half = divmod(h, 2)
            lanes = slice(p * LANES, (p + 1) * LANES)
            own = low_half if half == 0 else jnp.logical_not(low_half)

            def head_q(ref):
                q_pair = ref[:, lanes]
                return jnp.where(own, q_pair, jnp.zeros_like(q_pair))

            q_h = head_q(qh_ref)
            kt = kth_ref[0, lanes, pl.ds(ks, tq)]
            z = _dot(q_h, kt)
            if precise:
                z = z + (_dot(q_h, ktl_ref[0, lanes, pl.ds(ks, tq)]) + _dot(head_q(ql_ref), kt))
            z = z + bias_ref[h]
            sp = _softplus(z)
            if diagonal:
                sp = jnp.where(causal, sp, 0.0)
            carry = carry_ref[h]
            if precise:
                sp_hi, sp_lo = _split(sp)
                cum = _dot(sp_hi, tri) + _dot(sp_lo, tri)
            else:
                cum = _dot(sp.astype(BF16), tri)
            arg = z + jnp.concatenate([carry] * (tq // LANES), axis=1) - cum
            if diagonal:
                arg = jnp.where(causal, arg, NEG_BIG)
            w = jnp.exp(arg)
            v = vh_ref[pl.ds(ks, tq), lanes]
            if precise:
                w_hi, w_lo = _split(w)
                pv = _dot(w_hi, v) + (_dot(w_hi, vl_ref[pl.ds(ks, tq), lanes]) + _dot(w_lo, v))
            else:
                pv = _dot(w.astype(BF16), v)
            acc_ref[h] += pv
            carry_ref[h] = carry - jnp.broadcast_to(jnp.sum(sp, axis=1, keepdims=True), carry.shape)

    def sweep(precise):
        block(qi, True, precise)

        def body(j, _):
            block(qi - 1 - j, False, precise)
            return 0

        lax.fori_loop(0, qi, body, 0)

    @pl.when(qi < nq - 1)
    def _():
        sweep(False)

    @pl.when(qi == nq - 1)
    def _():
        sweep(True)

    for p in range(SB_HEADS // 2):
        o_ref[:, p * LANES:(p + 1) * LANES] = jnp.where(low_half, acc_ref[2 * p], acc_ref[2 * p + 1])


def _sb_prompt(bias, q_hi, q_lo, kt_hi, kt_lo, v_hi, v_lo, *, batch, seq):
    n, d_sb = q_hi.shape
    nq = seq // TQ
    qspec = pl.BlockSpec((TQ, d_sb), lambda b, i: (b * nq + i, 0))
    kspec = pl.BlockSpec((1, d_sb, seq), lambda b, i: (b, 0, 0))
    vspec = pl.BlockSpec((seq, d_sb), lambda b, i: (b, 0))
    return pl.pallas_call(
        _sb_prompt_kernel,
        grid=(batch, nq),
        in_specs=[pl.BlockSpec(memory_space=pltpu.SMEM), qspec, qspec, kspec, kspec, vspec, vspec],
        out_specs=qspec,
        out_shape=jax.ShapeDtypeStruct((n, d_sb), F32),
        scratch_shapes=[pltpu.VMEM((SB_HEADS, TQ, LANES), F32), pltpu.VMEM((SB_HEADS, TQ, LANES), F32)],
        compiler_params=_cparams(2),
        name="sb_prompt",
    )(bias, q_hi, q_lo, kt_hi, kt_lo, v_hi, v_lo)


def _sb_sample_kernel(pt_ref, q_ref, bias_ref, *refs, pp):
    del pt_ref
    k_refs, v_refs = refs[:pp], refs[pp:2 * pp]
    o_ref, carry_ref, acc_ref = refs[2 * pp:]
    j = pl.program_id(1)
    d_sb = SB_HEADS * SB_HEAD_DIM
    page = k_refs[0].shape[-1]

    @pl.when(j == 0)
    def _():
        carry_ref[...] = jnp.zeros_like(carry_ref)
        acc_ref[...] = jnp.zeros_like(acc_ref)

    head_of_lane = lax.broadcasted_iota(I32, (SB_HEADS, d_sb), 1) // SB_HEAD_DIM
    row = lax.broadcasted_iota(I32, (SB_HEADS, d_sb), 0)
    own = head_of_lane == row
    q_rows = jnp.broadcast_to(q_ref[0], (SB_HEADS, d_sb))
    q_bd = jnp.where(own, q_rows, 0.0)
    r = lax.broadcasted_iota(I32, (page, page), 0)
    c = lax.broadcasted_iota(I32, (page, page), 1)
    tri = jnp.where(r >= c, 1.0, 0.0).astype(BF16)

    def stacked_split(a):
        c = a * SPLIT_FACTOR
        hi = c - (c - a)
        return jnp.concatenate([hi, a - hi], axis=0).astype(BF16)

    def stack3(a, b, nt=False):
        b_hi, b_lo = _split(b)
        dot = _dot_nt if nt else _dot
        m = a.shape[0]
        both = dot(stacked_split(a), b_hi)
        return both[:m] + (both[m:] + dot(a.astype(BF16), b_lo))

    order = list(reversed(range(pp)))
    zs = [stack3(q_bd, k_refs[pg][0].reshape(d_sb, page)) + bias_ref[...] for pg in order]
    sps = [_softplus(z) for z in zs]
    totals = [jnp.broadcast_to(jnp.sum(sp, axis=1, keepdims=True), carry_ref.shape) for sp in sps]
    cums = []
    for sp in sps:
        both = _dot(stacked_split(sp), tri)
        cums.append(both[:SB_HEADS] + both[SB_HEADS:])
    carry = carry_ref[...]
    acc = acc_ref[...]
    for n, pg in enumerate(order):
        w = jnp.exp(zs[n] + carry - cums[n])
        vt = v_refs[pg][0].reshape(d_sb, page)
        acc = acc + stack3(w, vt, nt=True)
        carry = carry - totals[n]
    carry_ref[...] = carry
    acc_ref[...] = acc

    @pl.when(j == pl.num_programs(1) - 1)
    def _():
        o_ref[0] = jnp.sum(jnp.where(own, acc, 0.0), axis=0, keepdims=True)


def _sb_sample(page_table, q3, bias_b, kt_pages, vt_pages):
    db, n_pages = page_table.shape
    d_sb = q3.shape[-1]
    page = kt_pages.shape[-1]
    pp = PAGES_PER_STEP
    assert n_pages % pp == 0
    nchunk = n_pages // pp

    def page_spec(pg):
        return pl.BlockSpec((1, SB_HEADS, SB_HEAD_DIM, page),
                            lambda b, j, pt: (pt[b, (nchunk - 1 - j) * pp + pg], 0, 0, 0))

    grid_spec = pltpu.PrefetchScalarGridSpec(
        num_scalar_prefetch=1,
        grid=(db, nchunk),
        in_specs=[pl.BlockSpec((1, 1, d_sb), lambda b, j, pt: (b, 0, 0)),
                  pl.BlockSpec((SB_HEADS, page), lambda b, j, pt: (0, 0))]
        + [page_spec(pg) for pg in range(pp)] * 2,
        out_specs=pl.BlockSpec((1, 1, d_sb), lambda b, j, pt: (b, 0, 0)),
        scratch_shapes=[pltpu.VMEM((SB_HEADS, page), F32), pltpu.VMEM((SB_HEADS, d_sb), F32)],
    )
    return pl.pallas_call(
        functools.partial(_sb_sample_kernel, pp=pp),
        grid_spec=grid_spec,
        out_shape=jax.ShapeDtypeStruct((db, 1, d_sb), F32),
        compiler_params=_cparams(2),
        name="sb_sample",
    )(page_table, q3, bias_b, *([kt_pages] * pp), *([vt_pages] * pp))


def _route(hn2, wr_hi_ref, wr_lo_ref, rb_ref, dense):
    m = hn2.shape[0]
    x_hi, x_lo = _split(hn2)
    w_hi, w_lo = wr_hi_ref[...], wr_lo_ref[...]
    lg = (_dot_nt(w_hi, x_hi) + _dot_nt(w_lo, x_hi)) + (_dot_nt(w_hi, x_lo) + _dot_nt(w_lo, x_lo)) + rb_ref[...]

    g = [lg[k:k + 1, :] for k in range(N_EXPERT_GROUPS)]
    gmax = functools.reduce(jnp.maximum, g)
    gidx = jnp.full((1, m), N_EXPERT_GROUPS - 1, I32)
    for k in reversed(range(N_EXPERT_GROUPS - 1)):
        gidx = jnp.where(g[k] == gmax, k, gidx)
    p_group = 1.0 / functools.reduce(lambda a, b: a + b, [jnp.exp(gk - gmax) for gk in g])

    e = []
    for k in range(EXPERTS_PER_GROUP):
        last = 8 + (N_EXPERT_GROUPS - 1) * EXPERTS_PER_GROUP + k
        ek = lg[last:last + 1, :]
        for gi in reversed(range(N_EXPERT_GROUPS - 1)):
            row = 8 + gi * EXPERTS_PER_GROUP + k
            ek = jnp.where(gidx == gi, lg[row:row + 1, :], ek)
        e.append(ek)
    v1 = functools.reduce(jnp.maximum, e)
    i1 = jnp.full((1, m), EXPERTS_PER_GROUP - 1, I32)
    for k in reversed(range(EXPERTS_PER_GROUP - 1)):
        i1 = jnp.where(e[k] == v1, k, i1)
    e2 = [jnp.where(i1 == k, -jnp.inf, e[k]) for k in range(EXPERTS_PER_GROUP)]
    v2 = functools.reduce(jnp.maximum, e2)
    i2 = jnp.full((1, m), EXPERTS_PER_GROUP - 1, I32)
    for k in reversed(range(EXPERTS_PER_GROUP - 1)):
        i2 = jnp.where(e2[k] == v2, k, i2)
    t = jnp.exp(v2 - v1)
    w1 = p_group / (1.0 + t)
    w2 = p_group * t / (1.0 + t)
    rows = lax.broadcasted_iota(I32, (LANES, m), 0)

    if dense:
        e1 = gidx * EXPERTS_PER_GROUP + i1
        e2id = gidx * EXPERTS_PER_GROUP + i2
        coef_t = jnp.where(rows == e1, w1, 0.0) + jnp.where(rows == e2id, w2, 0.0)
        return None, coef_t.T

    first_low = i1 < i2
    ia = jnp.where(first_low, i1, i2)
    ib = jnp.where(first_low, i2, i1)
    ca = jnp.where(first_low, w1, w2)
    cb = jnp.where(first_low, w2, w1)
    pidx = jnp.zeros((1, m), I32)
    for n, (pa, pb) in enumerate(PAIRS):
        pidx = jnp.where(jnp.logical_and(ia == pa, ib == pb), n, pidx)
    bucket = gidx * len(PAIRS) + pidx
    coef_t = jnp.where(rows == 0, ca, jnp.where(rows == 1, cb, 0.0))
    return bucket, coef_t.T


def _ffn_norm_route_store(h1, gf_ref, wr_hi_ref, wr_lo_ref, rb_ref, h1_ref, xe_ref, bk_ref):
    d = h1.shape[1]
    hn2 = _rms(h1, gf_ref[...])
    bucket, coef = _route(hn2, wr_hi_ref, wr_lo_ref, rb_ref, dense=bk_ref is None)
    h1_ref[...] = h1
    xe_ref[:, :d] = hn2
    xe_ref[:, d:] = coef
    if bk_ref is not None:
        bk_ref[...] = bucket


def _pool_windows(xp_ref, inv_cnt):
    m = xp_ref.shape[0] - HALO_POOL
    gd = xp_ref.shape[1] // len(POOL_WINDOWS)
    outs = []
    for gi, w in enumerate(POOL_WINDOWS):
        lanes = slice(gi * gd, (gi + 1) * gd)
        s = xp_ref[HALO_POOL:HALO_POOL + m, lanes]
        cur = s
        for j in range(1, w):
            s = s + xp_ref[HALO_POOL - j:HALO_POOL - j + m, lanes]
        outs.append(s * inv_cnt(w) - cur)
    return outs


def _mix_ab_kernel(hp_ref, osbp_ref, up_ref, uprev_ref, hs_ref, osbs_ref, us_ref, hist_ref,
                   wpool_ref, wpool_lo_ref, pscale_ref, wout_ref, wout_lo_ref, gf_ref, wr_hi_ref, wr_lo_ref, rb_ref,
                   h1p_ref, xep_ref, bkp_ref, h1s_ref, xes_ref, xp_ref, *, npt, tps, pos0):
    i = pl.program_id(0)
    d_sb = osbp_ref.shape[1]

    def tail(pooled, osb, h, precise_osb):
        o_pool = jnp.concatenate(
            [_dot3(*_split(pooled[gi]), wpool_ref[gi], wpool_lo_ref[gi]) for gi in range(len(POOL_WINDOWS))],
            axis=1) * pscale_ref[...]
        mix = (_mm(osb, wout_ref, wout_lo_ref, precise_osb, rows=slice(0, d_sb))
               + _mm(o_pool, wout_ref, wout_lo_ref, True, rows=slice(d_sb, None)))
        return h + mix

    tin = i % tps

    def prompt_tile(precise_osb):
        xp_ref[:HALO_POOL, :] = jnp.where(tin == 0, 0.0, uprev_ref[...])
        xp_ref[HALO_POOL:, :] = up_ref[...]
        pos = tin * TM + lax.broadcasted_iota(I32, (TM, 1), 0)

        def inv_cnt(w):
            return 1.0 / jnp.minimum(pos + 1, w).astype(F32)

        h1 = tail(_pool_windows(xp_ref, inv_cnt), osbp_ref[...], hp_ref[...], precise_osb)
        _ffn_norm_route_store(h1, gf_ref, wr_hi_ref, wr_lo_ref, rb_ref, h1p_ref, xep_ref, bkp_ref)

    @pl.when(jnp.logical_and(i < npt, tin == tps - 1))
    def _():
        prompt_tile(True)

    @pl.when(jnp.logical_and(i < npt, tin != tps - 1))
    def _():
        prompt_tile(False)

    @pl.when(i == npt)
    def _():
        gd = us_ref.shape[1] // len(POOL_WINDOWS)
        db = hist_ref.shape[1]
        u = us_ref[...]
        pooled = []
        for gi, w in enumerate(POOL_WINDOWS):
            lanes = slice(gi * gd, (gi + 1) * gd)
            s = u[:db, lanes]
            for j in range(1, w):
                s = s + hist_ref[POOL_HIST - j][:, lanes]
            s = jnp.concatenate([s, u[db:, lanes]], axis=0)
            pooled.append(s * (1.0 / min(pos0 + 1, w)) - u[:, lanes])
        h1 = tail(pooled, osbs_ref[...], hs_ref[...], True)
        _ffn_norm_route_store(h1, gf_ref, wr_hi_ref, wr_lo_ref, rb_ref, h1s_ref, xes_ref, None)


def _mix_ab(h_p, osb_p, u_p, h_s, osb_s, u_s, hist, wpool, wpool_lo, pscale, wout, wout_lo, gf, wr_hi, wr_lo, rb, *, seq, pos0):
    n, d = h_p.shape
    npt = n // TM
    tps = seq // TM
    d_sb = osb_p.shape[1]
    d_pool = u_p.shape[1]
    hb = TM // HALO_POOL
    consts = (wpool, wpool_lo, pscale, wout, wout_lo, gf, wr_hi, wr_lo, rb)
    in_specs = [
        _pspec(d, npt), _pspec(d_sb, npt), _pspec(d_pool, npt),
        pl.BlockSpec((HALO_POOL, d_pool), lambda i: (jnp.maximum(jnp.minimum(i, npt - 1) * hb - 1, 0), 0)),
        _const_spec((SP, d)), _const_spec((SP, d_sb)), _const_spec((SP, d_pool)), _const_spec(hist.shape),
    ] + [_const_spec(c.shape) for c in consts]
    de = d + LANES
    out_shape = (
        jax.ShapeDtypeStruct((n, d), F32), jax.ShapeDtypeStruct((n, de), F32), jax.ShapeDtypeStruct((1, n), I32),
        jax.ShapeDtypeStruct((SP, d), F32), jax.ShapeDtypeStruct((SP, de), F32),
    )
    out_specs = (
        _pspec(d, npt), _pspec(de, npt), pl.BlockSpec((1, TM), lambda i: (0, jnp.minimum(i, npt - 1))),
        _const_spec((SP, d)), _const_spec((SP, de)),
    )
    return pl.pallas_call(
        functools.partial(_mix_ab_kernel, npt=npt, tps=tps, pos0=pos0),
        grid=(npt + 1,),
        in_specs=in_specs, out_specs=out_specs, out_shape=out_shape,
        scratch_shapes=[pltpu.VMEM((HALO_POOL + TM, d_pool), F32)],
        compiler_params=_cparams(),
        name="mix_ab",
    )(h_p, osb_p, u_p, u_p, h_s, osb_s, u_s, hist, *consts)


def _sort_kernel(b_ref, pos_ref, tiles_ref, carry_ref, start_ref):
    ps = pl.program_id(0)
    j = pl.program_id(1)
    cw = b_ref.shape[1]
    rows = lax.broadcasted_iota(I32, (NB_ROWS, cw), 0)
    onehot = rows == b_ref[...]
    oh = jnp.where(onehot, 1.0, 0.0)
    counts = jnp.broadcast_to(jnp.sum(oh, axis=1, keepdims=True), (NB_ROWS, LANES))

    @pl.when(jnp.logical_and(ps == 0, j == 0))
    def _():
        carry_ref[...] = jnp.zeros_like(carry_ref)

    @pl.when(ps == 0)
    def _():
        carry_ref[...] += counts

    @pl.when(jnp.logical_and(ps == 1, j == 0))
    def _():
        real = lax.broadcasted_iota(I32, (NB_ROWS, LANES), 0) < N_BUCKETS
        tiles = jnp.where(real, jnp.floor((carry_ref[...] + (TMX - 1)) * (1.0 / TMX)), 0.0)
        rr = lax.broadcasted_iota(I32, (NB_ROWS, NB_ROWS), 0)
        cc = lax.broadcasted_iota(I32, (NB_ROWS, NB_ROWS), 1)
        below = jnp.where(cc < rr, 1.0, 0.0).astype(BF16)
        start_ref[...] = _dot(below, tiles.astype(BF16)) * TMX
        tiles_ref[...] = tiles
        carry_ref[...] = jnp.zeros_like(carry_ref)

    @pl.when(ps == 1)
    def _():
        rj = lax.broadcasted_iota(I32, (cw, cw), 0)
        ct = lax.broadcasted_iota(I32, (cw, cw), 1)
        before = jnp.where(rj < ct, 1.0, 0.0).astype(BF16)
        rank = _dot(oh.astype(BF16), before)
        val = rank + carry_ref[:, 0:1] + start_ref[:, 0:1]
        pos_ref[...] = jnp.sum(jnp.where(onehot, val, 0.0), axis=0, keepdims=True).astype(I32)
        carry_ref[...] += counts


def _sort_positions(buckets):
    npad = buckets.shape[1]
    nch = npad // SORT_CHUNK
    return pl.pallas_call(
        _sort_kernel,
        grid=(2, nch),
        in_specs=[pl.BlockSpec((1, SORT_CHUNK), lambda ps, j: (0, j))],
        out_specs=(pl.BlockSpec((1, SORT_CHUNK), lambda ps, j: (0, j * ps)),
                   pl.BlockSpec((NB_ROWS, LANES), lambda ps, j: (0, 0))),
        out_shape=(jax.ShapeDtypeStruct((1, npad), I32), jax.ShapeDtypeStruct((NB_ROWS, LANES), F32)),
        scratch_shapes=[pltpu.VMEM((NB_ROWS, LANES), F32), pltpu.VMEM((NB_ROWS, LANES), F32)],
        compiler_params=_cparams(2),
        name="moe_sort",
    )(buckets)


def _move_rows_kernel(pos_ref, *refs, scatter):
    if scatter:
        tok_ref, _, sorted_ref, sems = refs
    else:
        sorted_ref, tok_ref, sems = refs
    nb = tok_ref.shape[0] // MOVE_BATCH

    def copy(tok_rows, sorted_rows, slot):
        src, dst = (tok_ref.at[tok_rows], sorted_ref.at[sorted_rows])
        if not scatter:
            src, dst = dst, src
        return pltpu.make_async_copy(src, dst, sems.at[slot])

    def batch_wait(slot):
        copy(pl.ds(0, MOVE_BATCH), pl.ds(0, MOVE_BATCH), slot).wait()

    def batch(b, _):
        slot = lax.rem(b, MOVE_SEMS)

        @pl.when(b >= MOVE_SEMS)
        def _():
            batch_wait(slot)

        def row(r, _):
            t = b * MOVE_BATCH + r
            copy(pl.ds(t, 1), pl.ds(pos_ref[0, t], 1), slot).start()
            return 0

        lax.fori_loop(0, MOVE_BATCH, row, 0, unroll=8)
        return 0

    lax.fori_loop(0, nb, batch, 0)

    def drain(b, _):
        batch_wait(lax.rem(b, MOVE_SEMS))
        return 0

    lax.fori_loop(max(nb - MOVE_SEMS, 0), nb, drain, 0)


def _scatter_rows(pos, src, dst):
    n, cols = src.shape
    chunk = min(MOVE_CHUNK, n)
    assert n % chunk == 0 and chunk % MOVE_BATCH == 0
    return pl.pallas_call(
        functools.partial(_move_rows_kernel, scatter=True),
        grid=(n // chunk,),
        in_specs=[pl.BlockSpec((1, chunk), lambda i: (0, i), memory_space=pltpu.SMEM),
                  pl.BlockSpec((chunk, cols), lambda i: (i, 0)),
                  pl.BlockSpec(memory_space=pl.ANY)],
        out_specs=pl.BlockSpec(memory_space=pl.ANY),
        out_shape=jax.ShapeDtypeStruct(dst.shape, dst.dtype),
        scratch_shapes=[pltpu.SemaphoreType.DMA((MOVE_SEMS,))],
        input_output_aliases={2: 0},
        compiler_params=_cparams(),
        name="moe_scatter",
    )(pos, src, dst)


def _gather_rows(pos, src):
    n = pos.shape[1]
    cols = src.shape[1]
    chunk = min(MOVE_CHUNK, n)
    assert n % chunk == 0 and chunk % MOVE_BATCH == 0
    return pl.pallas_call(
        functools.partial(_move_rows_kernel, scatter=False),
        grid=(n // chunk,),
        in_specs=[pl.BlockSpec((1, chunk), lambda i: (0, i), memory_space=pltpu.SMEM),
                  pl.BlockSpec(memory_space=pl.ANY)],
        out_specs=pl.BlockSpec((chunk, cols), lambda i: (i, 0)),
        out_shape=jax.ShapeDtypeStruct((n, cols), src.dtype),
        scratch_shapes=[pltpu.SemaphoreType.DMA((MOVE_SEMS,))],
        compiler_params=_cparams(),
        name="moe_gather",
    )(pos, src)


def _experts_kernel(ea_ref, eb_ref, valid_ref, xs_ref, wga_ref, wua_ref, wda_ref, wgb_ref, wub_ref, wdb_ref, ys_ref):
    i = pl.program_id(0)
    d = ys_ref.shape[1]

    @pl.when(valid_ref[i] == 1)
    def _():
        x = xs_ref[:, :d].astype(BF16)

        def expert(wg_ref, wu_ref, wd_ref):
            g = _dot(x, wg_ref[0].astype(BF16))
            u = _dot(x, wu_ref[0].astype(BF16))
            hid = g * _sigmoid(g) * u
            return _dot(hid.astype(BF16), wd_ref[0].astype(BF16))

        ya = expert(wga_ref, wua_ref, wda_ref)
        yb = expert(wgb_ref, wub_ref, wdb_ref)
        ys_ref[...] = xs_ref[:, d:d + 1] * ya + xs_ref[:, d + 1:d + 2] * yb

    @pl.when(valid_ref[i] == 0)
    def _():
        ys_ref[...] = jnp.zeros_like(ys_ref)


def _experts(ea, eb, valid, xs, w_gate, w_up, w_down):
    ns, de = xs.shape
    d = de - LANES
    dx = w_gate.shape[-1]
    ntiles = ns // TMX

    def wspec(shape, which):
        return pl.BlockSpec((1,) + shape, lambda i, ea, eb, valid: ((ea, eb)[which][i], 0, 0))

    grid_spec = pltpu.PrefetchScalarGridSpec(
        num_scalar_prefetch=3,
        grid=(ntiles,),
        in_specs=[pl.BlockSpec((TMX, de), lambda i, ea, eb, valid: (i, 0)),
                  wspec((d, dx), 0), wspec((d, dx), 0), wspec((dx, d), 0),
                  wspec((d, dx), 1), wspec((d, dx), 1), wspec((dx, d), 1)],
        out_specs=pl.BlockSpec((TMX, d), lambda i, ea, eb, valid: (i, 0)),
    )
    return pl.pallas_call(
        _experts_kernel,
        grid_spec=grid_spec,
        out_shape=jax.ShapeDtypeStruct((ns, d), F32),
        compiler_params=_cparams(),
        name="moe_experts",
    )(ea, eb, valid, xs, w_gate, w_up, w_down, w_gate, w_up, w_down)


def _moe_sparse(xe, bk, w_gate, w_up, w_down, layer):
    n_tok, de = xe.shape
    npad = -(-n_tok // SORT_CHUNK) * SORT_CHUNK
    buckets = jnp.concatenate([bk, jnp.full((1, npad - n_tok), N_BUCKETS, I32)], axis=1)
    pos, tiles = _sort_positions(buckets)

    ntiles = -(-n_tok // TMX) + N_BUCKETS
    tiles_per_bucket = tiles[:N_BUCKETS, 0].astype(I32)
    cum = jnp.cumsum(tiles_per_bucket)
    tile_ids = jnp.arange(ntiles, dtype=I32)
    valid = tile_ids < cum[-1]
    tb = jnp.sum((jnp.minimum(tile_ids, cum[-1] - 1)[:, None] >= cum[None, :]).astype(I32), axis=1)
    group, pidx = tb // len(PAIRS), tb % len(PAIRS)
    pa = jnp.asarray([p[0] for p in PAIRS], I32)[pidx]
    pb = jnp.asarray([p[1] for p in PAIRS], I32)[pidx]
    first = layer * N_EXPERTS + group * EXPERTS_PER_GROUP
    flat = lambda w: w.reshape((-1,) + w.shape[2:])

    pos = pos[:, :n_tok]
    xs = _scatter_rows(pos, xe, jnp.zeros((ntiles * TMX, de), F32))
    ys = _experts(first + pa, first + pb, valid.astype(I32), xs, flat(w_gate), flat(w_up), flat(w_down))
    return _gather_rows(pos, ys)


def _experts_dense_kernel(xe_ref, wg_ref, wu_ref, wd_ref, y_ref):
    e = pl.program_id(0)
    d = y_ref.shape[1]

    @pl.when(e == 0)
    def _():
        y_ref[...] = jnp.zeros_like(y_ref)

    x_hi, x_lo = _split(xe_ref[:, :d])
    g = _dot3(x_hi, x_lo, *_split(wg_ref[0, 0]))
    u = _dot3(x_hi, x_lo, *_split(wu_ref[0, 0]))
    hid = g * _sigmoid(g) * u
    y = _dot3(*_split(hid), *_split(wd_ref[0, 0]))
    lane = lax.broadcasted_iota(I32, (xe_ref.shape[0], LANES), 1)
    coef = jnp.sum(jnp.where(lane == e, xe_ref[:, d:], 0.0), axis=1, keepdims=True)
    y_ref[...] += coef * y


def _moe_dense(xe, w_gate, w_up, w_down, layer):
    m, de = xe.shape
    d = de - LANES
    _, n_exp, _, dx = w_gate.shape
    return pl.pallas_call(
        _experts_dense_kernel,
        grid=(n_exp,),
        in_specs=[_const_spec((m, de)),
                  pl.BlockSpec((1, 1, d, dx), lambda e: (layer, e, 0, 0)), pl.BlockSpec((1, 1, d, dx), lambda e: (layer, e, 0, 0)),
                  pl.BlockSpec((1, 1, dx, d), lambda e: (layer, e, 0, 0))],
        out_specs=_const_spec((m, d)),
        out_shape=jax.ShapeDtypeStruct((m, d), F32),
        compiler_params=_cparams(),
        name="moe_dense",
    )(xe, w_gate, w_up, w_down)


def _ple(h, y, p, gpg_ref, wpg_ref, wpg_lo_ref, bpg_ref, wple_ref, wple_lo_ref, precise):
    h2 = h + y
    gate = _sigmoid(_mm(_rms(h2, gpg_ref[...]), wpg_ref, wpg_lo_ref, precise) + bpg_ref[...])
    return h2 + gate * _mm(p, wple_ref, wple_lo_ref, precise)


def _ple_inproj_c_kernel(hp_ref, yp_ref, pp_ref, hs_ref, ys_ref, ps_ref,
                         gpg_ref, wpg_ref, wpg_lo_ref, bpg_ref, wple_ref, wple_lo_ref, gm_ref, win_ref, win_lo_ref,
                         h3p_ref, ucp_ref, h3s_ref, ucs_ref, *, npt):
    i = pl.program_id(0)

    def run(h_ref, y_ref, p_ref, h3_ref, uc_ref, precise):
        h3 = _ple(h_ref[...], y_ref[...], p_ref[...], gpg_ref, wpg_ref, wpg_lo_ref, bpg_ref, wple_ref, wple_lo_ref, precise)
        h3_ref[...] = h3
        ag = _mm(_rms(h3, gm_ref[...]), win_ref, win_lo_ref, precise)
        dc = ag.shape[1] // 2
        uc_ref[...] = ag[:, :dc] * _sigmoid(ag[:, dc:])

    @pl.when(i < npt)
    def _():
        run(hp_ref, yp_ref, pp_ref, h3p_ref, ucp_ref, False)

    @pl.when(i == npt)
    def _():
        run(hs_ref, ys_ref, ps_ref, h3s_ref, ucs_ref, True)


def _ple_inproj_c(h_p, y_p, p_all, layer, h_s, y_s, p_s, gpg, wpg, wpg_lo, bpg, wple, wple_lo, gm, win, win_lo):
    n, d = h_p.shape
    npt = n // TM
    dple = p_all.shape[1]
    dc = win.shape[1] // 2
    consts = (gpg, wpg, wpg_lo, bpg, wple, wple_lo, gm, win, win_lo)
    return pl.pallas_call(
        functools.partial(_ple_inproj_c_kernel, npt=npt),
        grid=(npt + 1,),
        in_specs=[_pspec(d, npt), _pspec(d, npt), _pspec(dple, npt, first_block=layer * npt),
                  _const_spec((SP, d)), _const_spec((SP, d)), _const_spec((SP, dple))]
        + [_const_spec(c.shape) for c in consts],
        out_specs=(_pspec(d, npt), _pspec(dc, npt), _const_spec((SP, d)), _const_spec((SP, dc))),
        out_shape=(jax.ShapeDtypeStruct((n, d), F32), jax.ShapeDtypeStruct((n, dc), F32),
                   jax.ShapeDtypeStruct((SP, d), F32), jax.ShapeDtypeStruct((SP, dc), F32)),
        compiler_params=_cparams(),
        name="ple_inproj_c",
    )(h_p, y_p, p_all, h_s, y_s, p_s, *consts)


def _ple_final_kernel(hp_ref, yp_ref, pp_ref, hs_ref, ys_ref, ps_ref,
                      gpg_ref, wpg_ref, wpg_lo_ref, bpg_ref, wple_ref, wple_lo_ref, gfin_ref, op_ref, os_ref, *, npt):
    i = pl.program_id(0)

    def run(h_ref, y_ref, p_ref, o_ref, precise):
        h = _ple(h_ref[...], y_ref[...], p_ref[...], gpg_ref, wpg_ref, wpg_lo_ref, bpg_ref, wple_ref, wple_lo_ref, precise)
        o_ref[...] = _rms(h, gfin_ref[...])

    @pl.when(i < npt)
    def _():
        run(hp_ref, yp_ref, pp_ref, op_ref, False)

    @pl.when(i == npt)
    def _():
        run(hs_ref, ys_ref, ps_ref, os_ref, True)


def _ple_final(h_p, y_p, p_all, layer, h_s, y_s, p_s, gpg, wpg, wpg_lo, bpg, wple, wple_lo, gfin):
    n, d = h_p.shape
    npt = n // TM
    dple = p_all.shape[1]
    consts = (gpg, wpg, wpg_lo, bpg, wple, wple_lo, gfin)
    return pl.pallas_call(
        functools.partial(_ple_final_kernel, npt=npt),
        grid=(npt + 1,),
        in_specs=[_pspec(d, npt), _pspec(d, npt), _pspec(dple, npt, first_block=layer * npt),
                  _const_spec((SP, d)), _const_spec((SP, d)), _const_spec((SP, dple))]
        + [_const_spec(c.shape) for c in consts],
        out_specs=(_pspec(d, npt), _const_spec((SP, d))),
        out_shape=(jax.ShapeDtypeStruct((n, d), F32), jax.ShapeDtypeStruct((SP, d), F32)),
        compiler_params=_cparams(),
        name="ple_final",
    )(h_p, y_p, p_all, h_s, y_s, p_s, *consts)


def _conv_tail(y, h, bdw_ref, lng_ref, lnb_ref, wout_ref, wout_lo_ref, precise):
    y = y + bdw_ref[...]
    mu = jnp.mean(y, axis=-1, keepdims=True)
    yc = y - mu
    yn = yc * lax.rsqrt(jnp.mean(yc * yc, axis=-1, keepdims=True) + EPS) * lng_ref[...] + lnb_ref[...]
    act = yn * _sigmoid(yn)
    return h + _mm(act, wout_ref, wout_lo_ref, precise)


def _mix_c_kernel(hp_ref, ucp_ref, ucprev_ref, hs_ref, ucs_ref, hist_ref,
                  wdw_ref, bdw_ref, lng_ref, lnb_ref, wout_ref, wout_lo_ref, gf_ref, wr_hi_ref, wr_lo_ref, rb_ref,
                  h1p_ref, xep_ref, bkp_ref, h1s_ref, xes_ref, xp_ref, sh_ref, wb_ref, y_ref, *, npt, tps):
    i = pl.program_id(0)

    @pl.when(i < npt)
    def _():
        tin = i % tps
        xp_ref[:HALO_CONV, :] = jnp.where(tin == 0, 0.0, ucprev_ref[...])
        xp_ref[HALO_CONV:, :] = ucp_ref[...]
        off = HALO_CONV - CONV_HIST
        half = TM // 2
        ext = sh_ref.shape[1]
        groups = CONV_ROWS // CONV_SUBLANES
        for j in range(CONV_WIDTH):
            wb_ref[j] = jnp.broadcast_to(wdw_ref[j:j + 1, :], wb_ref.shape[1:])
        for hb in range(TM // half):
            base = hb * half
            for s in range(1, CONV_SUBLANES):
                sh_ref[s - 1] = xp_ref[base + s:base + s + ext, :]

            def chunk(c, _):
                r0 = pl.multiple_of(c * CONV_ROWS, CONV_ROWS)
                accs = [None] * groups
                for j in range(CONV_WIDTH):
                    s, k8 = (off + j) % CONV_SUBLANES, (off + j) // CONV_SUBLANES * CONV_SUBLANES
                    wb = wb_ref[j]
                    for g in range(groups):
                        rows = pl.ds(r0 + k8 + g * CONV_SUBLANES, CONV_SUBLANES)
                        x = xp_ref[pl.ds(base + r0 + k8 + g * CONV_SUBLANES, CONV_SUBLANES), :] if s == 0 else sh_ref[s - 1, rows, :]
                        accs[g] = wb * x if accs[g] is None else accs[g] + wb * x
                for g in range(groups):
                    y_ref[pl.ds(base + r0 + g * CONV_SUBLANES, CONV_SUBLANES), :] = accs[g]
                return 0

            lax.fori_loop(0, half // CONV_ROWS, chunk, 0)
        h1 = _conv_tail(y_ref[...], hp_ref[...], bdw_ref, lng_ref, lnb_ref, wout_ref, wout_lo_ref, False)
        _ffn_norm_route_store(h1, gf_ref, wr_hi_ref, wr_lo_ref, rb_ref, h1p_ref, xep_ref, bkp_ref)

    @pl.when(i == npt)
    def _():
        db = hist_ref.shape[1]
        y = wdw_ref[CONV_WIDTH - 1:CONV_WIDTH, :] * ucs_ref[...]
        y_hist = wdw_ref[0:1, :] * hist_ref[0]
        for j in range(1, CONV_HIST):
            y_hist = y_hist + wdw_ref[j:j + 1, :] * hist_ref[j]
        y = jnp.concatenate([y[:db] + y_hist, y[db:]], axis=0)
        h1 = _conv_tail(y, hs_ref[...], bdw_ref, lng_ref, lnb_ref, wout_ref, wout_lo_ref, True)
        _ffn_norm_route_store(h1, gf_ref, wr_hi_ref, wr_lo_ref, rb_ref, h1s_ref, xes_ref, None)


def _mix_c(h_p, uc_p, h_s, uc_s, hist, wdw, bdw, lng, lnb, wout, wout_lo, gf, wr_hi, wr_lo, rb, *, seq):
    n, d = h_p.shape
    npt = n // TM
    tps = seq // TM
    dc = uc_p.shape[1]
    hb = TM // HALO_CONV
    shifted_reach = (HALO_CONV - CONV_HIST + CONV_WIDTH - 2) // CONV_SUBLANES * CONV_SUBLANES
    assert (TM // 2) % CONV_ROWS == 0 and CONV_ROWS % CONV_SUBLANES == 0
    consts = (wdw, bdw, lng, lnb, wout, wout_lo, gf, wr_hi, wr_lo, rb)
    in_specs = [
        _pspec(d, npt), _pspec(dc, npt),
        pl.BlockSpec((HALO_CONV, dc), lambda i: (jnp.maximum(jnp.minimum(i, npt - 1) * hb - 1, 0), 0)),
        _const_spec((SP, d)), _const_spec((SP, dc)), _const_spec(hist.shape),
    ] + [_const_spec(c.shape) for c in consts]
    de = d + LANES
    out_shape = (
        jax.ShapeDtypeStruct((n, d), F32), jax.ShapeDtypeStruct((n, de), F32), jax.ShapeDtypeStruct((1, n), I32),
        jax.ShapeDtypeStruct((SP, d), F32), jax.ShapeDtypeStruct((SP, de), F32),
    )
    out_specs = (
        _pspec(d, npt), _pspec(de, npt), pl.BlockSpec((1, TM), lambda i: (0, jnp.minimum(i, npt - 1))),
        _const_spec((SP, d)), _const_spec((SP, de)),
    )
    return pl.pallas_call(
        functools.partial(_mix_c_kernel, npt=npt, tps=tps),
        grid=(npt + 1,),
        in_specs=in_specs, out_specs=out_specs, out_shape=out_shape,
        scratch_shapes=[pltpu.VMEM((HALO_CONV + TM, dc), F32),
                        pltpu.VMEM((CONV_SUBLANES - 1, TM // 2 + shifted_reach, dc), F32),
                        pltpu.VMEM((CONV_WIDTH, CONV_SUBLANES, dc), F32),
                        pltpu.VMEM((TM, dc), F32)],
        compiler_params=_cparams(),
        name="mix_c",
    )(h_p, uc_p, uc_p, h_s, uc_s, hist, *consts)


def _router_weights(w_rg, b_rg, w_re, b_re):
    d = w_rg.shape[0]
    wt = jnp.zeros((ROUTE_ROWS, d), F32).at[:N_EXPERT_GROUPS].set(w_rg.T).at[8:8 + w_re.shape[1]].set(w_re.T)
    hi, lo = _hi_lo(wt)
    rb = jnp.zeros((ROUTE_ROWS, 1), F32).at[:N_EXPERT_GROUPS, 0].set(b_rg).at[8:8 + w_re.shape[1], 0].set(b_re)
    return hi, lo, rb


def _pad_rows(x, rows):
    return jnp.concatenate([x, jnp.zeros((rows - x.shape[0],) + x.shape[1:], x.dtype)], axis=0)


def kernel(x_prompt, x_sample, p_prompt, p_sample, cache_k, cache_v, state_pool, state_conv, page_table, norm_mix, norm_ffn, norm_final, w_in_ab, sb_bias, w_pool, pool_scale, w_out_ab, w_in_c, w_dw, b_dw, ln_g_c, ln_b_c, w_out_c, w_rg, b_rg, w_re, b_re, w_gate_e, w_up_e, w_down_e, w_ple, w_ple_gate, b_ple_gate, g_ple_gate):
    batch, seq, d = x_prompt.shape
    db, dec_seq, _ = x_sample.shape
    depth = norm_mix.shape[0]
    d_sb = SB_HEADS * SB_HEAD_DIM
    n_pages = page_table.shape[1]
    page = cache_k.shape[2]
    pos0 = n_pages * page
    assert dec_seq == 1 and depth == 2 and db <= SP and db % 8 == 0 and w_gate_e.shape[1] == N_EXPERTS
    assert seq % TM == 0 and seq % TQ == 0 and TMX & (TMX - 1) == 0
    assert max(CONV_HIST, POOL_HIST) <= TQ <= TM
    n_p = batch * seq

    row = lambda v: v.reshape(1, -1)
    x_p = x_prompt.reshape(n_p, d)
    x_s = _pad_rows(x_sample.reshape(db, d), SP)
    p_all = p_prompt.reshape(depth * n_p, -1)
    ps = [_pad_rows(p_sample[i].reshape(db, -1), SP) for i in range(depth)]

    w_in, w_in_lo = _hi_lo(w_in_ab[0])
    w_kv_t, w_kv_t_lo = _hi_lo(w_in_ab[0][:, d_sb:3 * d_sb].T)
    qh_p, ql_p, vh_p, vl_p, u_p, kt_p, vt_p, kth_p, ktl_p, q_s, u_s, k_s, v_s = _inproj_ab(
        x_p, x_s, row(norm_mix[0]), w_in, w_in_lo, w_kv_t, w_kv_t_lo[:d_sb], batch=batch, seq=seq)
    osb_p = _sb_prompt(sb_bias[0], qh_p, ql_p, kth_p, ktl_p, vh_p, vl_p, batch=batch, seq=seq)
    kt_pages = jnp.transpose(cache_k[0], (0, 2, 3, 1))
    vt_pages = jnp.transpose(cache_v[0], (0, 2, 3, 1))
    bias_b = jnp.broadcast_to(sb_bias[0][:, None], (SB_HEADS, page))
    osb_s = _sb_sample(page_table, q_s[:db].reshape(db, 1, d_sb), bias_b, kt_pages, vt_pages)
    osb_s = _pad_rows(osb_s.reshape(db, d_sb), SP)

    pool_hist = jnp.transpose(state_pool[0], (1, 0, 2))
    wr_hi, wr_lo, rb = _router_weights(w_rg[0], b_rg[0], w_re[0], b_re[0])
    w_pool_hi, w_pool_lo = _hi_lo(w_pool[0])
    w_out_hi, w_out_lo = _hi_lo(w_out_ab[0])
    h1_p, xe_p, bk_p, h1_s, xe_s = _mix_ab(
        x_p, osb_p, u_p, x_s, osb_s, u_s, pool_hist, w_pool_hi, w_pool_lo, row(pool_scale[0]),
        w_out_hi, w_out_lo, row(norm_ffn[0]), wr_hi, wr_lo, rb, seq=seq, pos0=pos0)
    y_p = _moe_sparse(xe_p, bk_p, w_gate_e, w_up_e, w_down_e, 0)
    y_s = _moe_dense(xe_s, w_gate_e, w_up_e, w_down_e, 0)

    w_pg, w_pg_lo = _hi_lo(w_ple_gate[0])
    w_pl, w_pl_lo = _hi_lo(w_ple[0])
    w_ic, w_ic_lo = _hi_lo(w_in_c[0])
    h3_p, uc_p, h3_s, uc_s = _ple_inproj_c(
        h1_p, y_p, p_all, 0, h1_s, y_s, ps[0], row(g_ple_gate[0]), w_pg, w_pg_lo, row(b_ple_gate[0]),
        w_pl, w_pl_lo, row(norm_mix[1]), w_ic, w_ic_lo)
    conv_hist = jnp.transpose(state_conv[0], (1, 0, 2))
    wr_hi, wr_lo, rb = _router_weights(w_rg[1], b_rg[1], w_re[1], b_re[1])
    w_oc, w_oc_lo = _hi_lo(w_out_c[0])
    h4_p, xe_p, bk_p, h4_s, xe_s = _mix_c(
        h3_p, uc_p, h3_s, uc_s, conv_hist, w_dw[0], row(b_dw[0]), row(ln_g_c[0]), row(ln_b_c[0]),
        w_oc, w_oc_lo, row(norm_ffn[1]), wr_hi, wr_lo, rb, seq=seq)
    y_p = _moe_sparse(xe_p, bk_p, w_gate_e, w_up_e, w_down_e, 1)
    y_s = _moe_dense(xe_s, w_gate_e, w_up_e, w_down_e, 1)
    w_pg, w_pg_lo = _hi_lo(w_ple_gate[1])
    w_pl, w_pl_lo = _hi_lo(w_ple[1])
    out_p, out_s = _ple_final(
        h4_p, y_p, p_all, 1, h4_s, y_s, ps[1], row(g_ple_gate[1]), w_pg, w_pg_lo, row(b_ple_gate[1]),
        w_pl, w_pl_lo, row(norm_final))

    def heads_last(t):
        return jnp.transpose(t.reshape(batch, SB_HEADS, SB_HEAD_DIM, seq), (0, 3, 1, 2))[None]

    y_prompt = out_p.reshape(batch, seq, d)
    y_sample = out_s[:db].reshape(db, 1, d)
    k_prompt, v_prompt = heads_last(kt_p), heads_last(vt_p)
    pool_prompt = u_p.reshape(batch, seq, -1)[:, seq - POOL_HIST:][None]
    conv_prompt = uc_p.reshape(batch, seq, -1)[:, seq - CONV_HIST:][None]
    k_sample = k_s[:db].reshape(1, db, 1, SB_HEADS, SB_HEAD_DIM)
    v_sample = v_s[:db].reshape(1, db, 1, SB_HEADS, SB_HEAD_DIM)
    pool_sample = jnp.transpose(jnp.concatenate([pool_hist[1:], u_s[None, :db]], axis=0), (1, 0, 2))[None]
    conv_sample = jnp.transpose(jnp.concatenate([conv_hist[1:], uc_s[None, :db]], axis=0), (1, 0, 2))[None]
    return (y_prompt, y_sample, k_prompt, v_prompt, pool_prompt, conv_prompt, k_sample, v_sample, pool_sample, conv_sample)
```

```python
import functools

import jax
import jax.numpy as jnp
from jax import lax
from jax.experimental import pallas as pl
from jax.experimental.pallas import tpu as pltpu

F32 = jnp.float32
BF16 = jnp.bfloat16
I32 = jnp.int32

EPS = 1e-6
SB_HEADS = 8
SB_HEAD_DIM = 64
POOL_WINDOWS = (2, 4, 8, 16)
POOL_HIST = max(POOL_WINDOWS) - 1
CONV_WIDTH = 31
CONV_HIST = CONV_WIDTH - 1
N_EXPERT_GROUPS = 4
EXPERTS_PER_GROUP = 4
N_EXPERTS = N_EXPERT_GROUPS * EXPERTS_PER_GROUP
PAIRS = ((0, 1), (0, 2), (0, 3), (1, 3), (1, 2), (2, 3))
N_BUCKETS = N_EXPERT_GROUPS * len(PAIRS)

LANES = 128
TM = 512
SP = 128
TQ = 256
TMX = 256
SORT_CHUNK = 512
NB_ROWS = 32
ROUTE_ROWS = 32
HALO_POOL = 16
HALO_CONV = 32
CONV_SUBLANES = 8
CONV_ROWS = 32
PAGES_PER_STEP = 16
MOVE_CHUNK = 2048
MOVE_BATCH = 128
MOVE_SEMS = 8
NEG_BIG = -1e30
LOG2E = 1.4426950408889634
SPLIT_FACTOR = 65537.0
VMEM_LIMIT = 56 * 1024 * 1024


def _cparams(n_axes=1, vmem=VMEM_LIMIT):
    return pltpu.CompilerParams(dimension_semantics=("arbitrary",) * n_axes, vmem_limit_bytes=vmem)


def _rms(x, g):
    return x * lax.rsqrt(jnp.mean(x * x, axis=-1, keepdims=True) + EPS) * g


def _sigmoid(x):
    return 1.0 / (1.0 + jnp.exp(-x))


def _softplus(z):
    return jnp.maximum(z, 0.0) + jnp.log(1.0 + jnp.exp(-jnp.abs(z)))


def _dot(a, b):
    return jnp.dot(a, b, preferred_element_type=F32)


def _dot_nt(a, b):
    return lax.dot_general(a, b, (((1,), (1,)), ((), ())), preferred_element_type=F32)


def _split(a):
    c = a * SPLIT_FACTOR
    hi = c - (c - a)
    return hi.astype(BF16), (a - hi).astype(BF16)


def _dot3(a_hi, a_lo, w_hi, w_lo):
    return _dot(a_hi, w_hi) + (_dot(a_hi, w_lo) + _dot(a_lo, w_hi))


def _mm(a, w_ref, wlo_ref, precise, rows=slice(None), cols=slice(None)):
    if precise:
        return _dot3(*_split(a), w_ref[rows, cols], wlo_ref[rows, cols])
    return _dot(a.astype(BF16), w_ref[rows, cols])


def _hi_lo(w):
    return _split(w)


def _pspec(cols, npt, rows=TM, first_block=0):
    return pl.BlockSpec((rows, cols), lambda i: (first_block + jnp.minimum(i, npt - 1), 0))


def _const_spec(shape):
    nd = len(shape)
    return pl.BlockSpec(shape, lambda i: (0,) * nd)


def _inproj_ab_kernel(xp_ref, xs_ref, g_ref, w_ref, wlo_ref, wt_ref, wtlo_ref,
                      qhp_ref, qlp_ref, vhp_ref, vlp_ref, up_ref, ktp_ref, vtp_ref, khtp_ref, kltp_ref,
                      qs_ref, us_ref, ks_ref, vs_ref, *, npt, tps, d_sb):
    i = pl.program_id(0)
    ucols = slice(3 * d_sb, None)
    last_in_seq = i % tps == tps - 1

    def prompt_tile(precise_q):
        hn, hn_lo = _split(_rms(xp_ref[...], g_ref[...]))

        def proj(cols):
            return _dot3(hn, hn_lo, w_ref[:, cols], wlo_ref[:, cols])

        qcols = slice(0, d_sb)
        q = (proj(qcols) if precise_q else _dot(hn, w_ref[:, qcols])) * (SB_HEAD_DIM ** -0.5)
        qhp_ref[...], qlp_ref[...] = _split(q)
        vu = proj(slice(2 * d_sb, None))
        vhp_ref[...], vlp_ref[...] = _split(vu[:, :d_sb])
        up_ref[...] = vu[:, d_sb:]
        kt = _dot_nt(wt_ref[:d_sb], hn) + (_dot_nt(wtlo_ref[...], hn) + _dot_nt(wt_ref[:d_sb], hn_lo))
        ktp_ref[0] = kt
        khtp_ref[0], kltp_ref[0] = _split(kt)
        vtp_ref[0] = _dot_nt(wt_ref[d_sb:], hn)

    @pl.when(jnp.logical_and(i < npt, last_in_seq))
    def _():
        prompt_tile(True)

    @pl.when(jnp.logical_and(i < npt, jnp.logical_not(last_in_seq)))
    def _():
        prompt_tile(False)

    @pl.when(i == npt)
    def _():
        pr = _mm(_rms(xs_ref[...], g_ref[...]), w_ref, wlo_ref, True)
        qs_ref[...] = pr[:, :d_sb] * (SB_HEAD_DIM ** -0.5)
        ks_ref[...] = pr[:, d_sb:2 * d_sb]
        vs_ref[...] = pr[:, 2 * d_sb:3 * d_sb]
        us_ref[...] = pr[:, ucols]


def _inproj_ab(x_p, x_s, g, w, w_lo, wt, wt_lo, *, batch, seq):
    n, d = x_p.shape
    npt = n // TM
    tps = seq // TM
    d_sb = SB_HEADS * SB_HEAD_DIM
    d_pool = w.shape[1] - 3 * d_sb
    tspec = pl.BlockSpec((1, d_sb, TM), lambda i: (jnp.minimum(i, npt - 1) // tps, 0, jnp.minimum(i, npt - 1) % tps))
    out_shape = (
        jax.ShapeDtypeStruct((n, d_sb), BF16),
        jax.ShapeDtypeStruct((n, d_sb), BF16),
        jax.ShapeDtypeStruct((n, d_sb), BF16),
        jax.ShapeDtypeStruct((n, d_sb), BF16),
        jax.ShapeDtypeStruct((n, d_pool), F32),
        jax.ShapeDtypeStruct((batch, d_sb, seq), F32),
        jax.ShapeDtypeStruct((batch, d_sb, seq), F32),
        jax.ShapeDtypeStruct((batch, d_sb, seq), BF16),
        jax.ShapeDtypeStruct((batch, d_sb, seq), BF16),
        jax.ShapeDtypeStruct((SP, d_sb), F32),
        jax.ShapeDtypeStruct((SP, d_pool), F32),
        jax.ShapeDtypeStruct((SP, d_sb), F32),
        jax.ShapeDtypeStruct((SP, d_sb), F32),
    )
    out_specs = (
        _pspec(d_sb, npt), _pspec(d_sb, npt), _pspec(d_sb, npt), _pspec(d_sb, npt), _pspec(d_pool, npt),
        tspec, tspec, tspec, tspec,
        _const_spec((SP, d_sb)), _const_spec((SP, d_pool)), _const_spec((SP, d_sb)), _const_spec((SP, d_sb)),
    )
    consts = (g, w, w_lo, wt, wt_lo)
    return pl.pallas_call(
        functools.partial(_inproj_ab_kernel, npt=npt, tps=tps, d_sb=d_sb),
        grid=(npt + 1,),
        in_specs=[_pspec(d, npt), _const_spec((SP, d))] + [_const_spec(c.shape) for c in consts],
        out_specs=out_specs,
        out_shape=out_shape,
        compiler_params=_cparams(),
        name="inproj_ab",
    )(x_p, x_s, *consts)


def _sb_prompt_kernel(bias_ref, qh_ref, ql_ref, kth_ref, ktl_ref, vh_ref, vl_ref, o_ref, carry_ref, acc_ref):
    qi = pl.program_id(1)
    nq = pl.num_programs(1)
    tq = qh_ref.shape[0]
    rk = lax.broadcasted_iota(I32, (tq, tq), 0)
    ck = lax.broadcasted_iota(I32, (tq, tq), 1)
    tri = jnp.where(rk >= ck, 1.0, 0.0).astype(BF16)

    carry_ref[...] = jnp.zeros_like(carry_ref)
    acc_ref[...] = jnp.zeros_like(acc_ref)

    low_half = lax.broadcasted_iota(I32, (tq, LANES), 1) < SB_HEAD_DIM
    causal = ck < rk

    def block(kj, diagonal, precise):
        ks = kj * tq if isinstance(kj, int) else pl.multiple_of(kj * tq, tq)
        for h in range(SB_HEADS):
            p, half = divmod(h, 2)
            lanes = slice(p * LANES, (p + 1) * LANES)
            own = low_half if half == 0 else jnp.logical_not(low_half)

            def head_q(ref):
                q_pair = ref[:, lanes]
                return jnp.where(own, q_pair, jnp.zeros_like(q_pair))

            q_h = head_q(qh_ref)
            kt = kth_ref[0, lanes, pl.ds(ks, tq)]
            z = _dot(q_h, kt)
            if precise:
                z = z + (_dot(q_h, ktl_ref[0, lanes, pl.ds(ks, tq)]) + _dot(head_q(ql_ref), kt))
            z = z + bias_ref[h]
            sp = jnp.maximum(z, 0.0) + jnp.log(1.0 + jnp.exp2(jnp.abs(z) * (-LOG2E)))
            if diagonal:
                sp = jnp.where(causal, sp, 0.0)
            carry = carry_ref[h]
            if precise:
                sp_hi, sp_lo = _split(sp)
                cum = _dot(sp_hi, tri) + _dot(sp_lo, tri)
            else:
                cum = _dot(sp.astype(BF16), tri)
            arg = z - cum
            if diagonal:
                arg = jnp.where(causal, arg, NEG_BIG)
            w = jnp.exp(arg)
            v = vh_ref[pl.ds(ks, tq), lanes]
            if precise:
                w_hi, w_lo = _split(w)
                pv = _dot(w_hi, v) + (_dot(w_hi, vl_ref[pl.ds(ks, tq), lanes]) + _dot(w_lo, v))
            else:
                pv = _dot(w.astype(BF16), v)
            acc_ref[h] += jnp.exp(carry) * pv
            carry_ref[h] = carry - jnp.broadcast_to(jnp.sum(sp, axis=1, keepdims=True), carry.shape)

    def sweep(precise):
        block(qi, True, precise)
        if precise:
            def body(j, _):
                block(qi - 1 - j, False, precise)
                return 0

            lax.fori_loop(0, qi, body, 0)
            return

        def body2(j, _):
            block(qi - 1 - 2 * j, False, precise)
            block(qi - 2 - 2 * j, False, precise)
            return 0

        lax.fori_loop(0, qi // 2, body2, 0)

        @pl.when(qi % 2 == 1)
        def _():
            block(0, False, precise)

    @pl.when(qi < nq - 1)
    def _():
        sweep(False)

    @pl.when(qi == nq - 1)
    def _():
        sweep(True)

    for p in range(SB_HEADS // 2):
        o_ref[:, p * LANES:(p + 1) * LANES] = jnp.where(low_half, acc_ref[2 * p], acc_ref[2 * p + 1])


def _sb_prompt(bias, q_hi, q_lo, kt_hi, kt_lo, v_hi, v_lo, *, batch, seq):
    n, d_sb = q_hi.shape
    nq = seq // TQ
    qspec = pl.BlockSpec((TQ, d_sb), lambda b, i: (b * nq + i, 0))
    kspec = pl.BlockSpec((1, d_sb, seq), lambda b, i: (b, 0, 0))
    vspec = pl.BlockSpec((seq, d_sb), lambda b, i: (b, 0))
    return pl.pallas_call(
        _sb_prompt_kernel,
        grid=(batch, nq),
        in_specs=[pl.BlockSpec(memory_space=pltpu.SMEM), qspec, qspec, kspec, kspec, vspec, vspec],
        out_specs=qspec,
        out_shape=jax.ShapeDtypeStruct((n, d_sb), F32),
        scratch_shapes=[pltpu.VMEM((SB_HEADS, TQ, LANES), F32), pltpu.VMEM((SB_HEADS, TQ, LANES), F32)],
        compiler_params=_cparams(2),
        name="sb_prompt",
    )(bias, q_hi, q_lo, kt_hi, kt_lo, v_hi, v_lo)


def _sb_sample_kernel(pt_ref, q_ref, bias_ref, *refs, pp):
    del pt_ref
    k_refs, v_refs = refs[:pp], refs[pp:2 * pp]
    o_ref, carry_ref, acc_ref = refs[2 * pp:]
    j = pl.program_id(1)
    d_sb = SB_HEADS * SB_HEAD_DIM
    page = k_refs[0].shape[-1]

    @pl.when(j == 0)
    def _():
        carry_ref[...] = jnp.zeros_like(carry_ref)
        acc_ref[...] = jnp.zeros_like(acc_ref)

    head_of_lane = lax.broadcasted_iota(I32, (SB_HEADS, d_sb), 1) // SB_HEAD_DIM
    row = lax.broadcasted_iota(I32, (SB_HEADS, d_sb), 0)
    own = head_of_lane == row
    q_rows = jnp.broadcast_to(q_ref[0], (SB_HEADS, d_sb))
    q_bd = jnp.where(own, q_rows, 0.0)
    r = lax.broadcasted_iota(I32, (page, page), 0)
    c = lax.broadcasted_iota(I32, (page, page), 1)
    tri = jnp.where(r >= c, 1.0, 0.0).astype(BF16)

    def stacked_split(a):
        c = a * SPLIT_FACTOR
        hi = c - (c - a)
        return jnp.concatenate([hi, a - hi], axis=0).astype(BF16)

    def stack3(a, b, nt=False):
        b_hi, b_lo = _split(b)
        dot = _dot_nt if nt else _dot
        m = a.shape[0]
        both = dot(stacked_split(a), b_hi)
        return both[:m] + (both[m:] + dot(a.astype(BF16), b_lo))

    order = list(reversed(range(pp)))
    zs = [stack3(q_bd, k_refs[pg][0].reshape(d_sb, page)) + bias_ref[...] for pg in order]
    sps = [_softplus(z) for z in zs]
    totals = [jnp.broadcast_to(jnp.sum(sp, axis=1, keepdims=True), carry_ref.shape) for sp in sps]
    cums = []
    for sp in sps:
        both = _dot(stacked_split(sp), tri)
        cums.append(both[:SB_HEADS] + both[SB_HEADS:])
    carry = carry_ref[...]
    acc = acc_ref[...]
    for n, pg in enumerate(order):
        w = jnp.exp(zs[n] + carry - cums[n])
        vt = v_refs[pg][0].reshape(d_sb, page)
        acc = acc + stack3(w, vt, nt=True)
        carry = carry - totals[n]
    carry_ref[...] = carry
    acc_ref[...] = acc

    @pl.when(j == pl.num_programs(1) - 1)
    def _():
        o_ref[0] = jnp.sum(jnp.where(own, acc, 0.0), axis=0, keepdims=True)


def _sb_sample(page_table, q3, bias_b, kt_pages, vt_pages):
    db, n_pages = page_table.shape
    d_sb = q3.shape[-1]
    page = kt_pages.shape[-1]
    pp = PAGES_PER_STEP
    assert n_pages % pp == 0
    nchunk = n_pages // pp

    def page_spec(pg):
        return pl.BlockSpec((1, SB_HEADS, SB_HEAD_DIM, page),
                            lambda b, j, pt: (pt[b, (nchunk - 1 - j) * pp + pg], 0, 0, 0))

    grid_spec = pltpu.PrefetchScalarGridSpec(
        num_scalar_prefetch=1,
        grid=(db, nchunk),
        in_specs=[pl.BlockSpec((1, 1, d_sb), lambda b, j, pt: (b, 0, 0)),
                  pl.BlockSpec((SB_HEADS, page), lambda b, j, pt: (0, 0))]
        + [page_spec(pg) for pg in range(pp)] * 2,
        out_specs=pl.BlockSpec((1, 1, d_sb), lambda b, j, pt: (b, 0, 0)),
        scratch_shapes=[pltpu.VMEM((SB_HEADS, page), F32), pltpu.VMEM((SB_HEADS, d_sb), F32)],
    )
    return pl.pallas_call(
        functools.partial(_sb_sample_kernel, pp=pp),
        grid_spec=grid_spec,
        out_shape=jax.ShapeDtypeStruct((db, 1, d_sb), F32),
        compiler_params=_cparams(2),
        name="sb_sample",
    )(page_table, q3, bias_b, *([kt_pages] * pp), *([vt_pages] * pp))


def _route(hn2, wr_hi_ref, wr_lo_ref, rb_ref, dense):
    m = hn2.shape[0]
    x_hi, x_lo = _split(hn2)
    w_hi, w_lo = wr_hi_ref[...], wr_lo_ref[...]
    lg = (_dot_nt(w_hi, x_hi) + _dot_nt(w_lo, x_hi)) + (_dot_nt(w_hi, x_lo) + _dot_nt(w_lo, x_lo)) + rb_ref[...]

    g = [lg[k:k + 1, :] for k in range(N_EXPERT_GROUPS)]
    gmax = functools.reduce(jnp.maximum, g)
    gidx = jnp.full((1, m), N_EXPERT_GROUPS - 1, I32)
    for k in reversed(range(N_EXPERT_GROUPS - 1)):
        gidx = jnp.where(g[k] == gmax, k, gidx)
    p_group = 1.0 / functools.reduce(lambda a, b: a + b, [jnp.exp(gk - gmax) for gk in g])

    e = []
    for k in range(EXPERTS_PER_GROUP):
        last = 8 + (N_EXPERT_GROUPS - 1) * EXPERTS_PER_GROUP + k
        ek = lg[last:last + 1, :]
        for gi in reversed(range(N_EXPERT_GROUPS - 1)):
            row = 8 + gi * EXPERTS_PER_GROUP + k
            ek = jnp.where(gidx == gi, lg[row:row + 1, :], ek)
        e.append(ek)
    v1 = functools.reduce(jnp.maximum, e)
    i1 = jnp.full((1, m), EXPERTS_PER_GROUP - 1, I32)
    for k in reversed(range(EXPERTS_PER_GROUP - 1)):
        i1 = jnp.where(e[k] == v1, k, i1)
    e2 = [jnp.where(i1 == k, -jnp.inf, e[k]) for k in range(EXPERTS_PER_GROUP)]
    v2 = functools.reduce(jnp.maximum, e2)
    i2 = jnp.full((1, m), EXPERTS_PER_GROUP - 1, I32)
    for k in reversed(range(EXPERTS_PER_GROUP - 1)):
        i2 = jnp.where(e2[k] == v2, k, i2)
    t = jnp.exp(v2 - v1)
    w1 = p_group / (1.0 + t)
    w2 = p_group * t / (1.0 + t)
    rows = lax.broadcasted_iota(I32, (LANES, m), 0)

    if dense:
        e1 = gidx * EXPERTS_PER_GROUP + i1
        e2id = gidx * EXPERTS_PER_GROUP + i2
        coef_t = jnp.where(rows == e1, w1, 0.0) + jnp.where(rows == e2id, w2, 0.0)
        return None, coef_t.T

    first_low = i1 < i2
    ia = jnp.where(first_low, i1, i2)
    ib = jnp.where(first_low, i2, i1)
    ca = jnp.where(first_low, w1, w2)
    cb = jnp.where(first_low, w2, w1)
    pidx = jnp.zeros((1, m), I32)
    for n, (pa, pb) in enumerate(PAIRS):
        pidx = jnp.where(jnp.logical_and(ia == pa, ib == pb), n, pidx)
    bucket = gidx * len(PAIRS) + pidx
    coef_t = jnp.where(rows == 0, ca, jnp.where(rows == 1, cb, 0.0))
    return bucket, coef_t.T


def _ffn_norm_route_store(h1, gf_ref, wr_hi_ref, wr_lo_ref, rb_ref, h1_ref, xe_ref, bk_ref):
    d = h1.shape[1]
    hn2 = _rms(h1, gf_ref[...])
    bucket, coef = _route(hn2, wr_hi_ref, wr_lo_ref, rb_ref, dense=bk_ref is None)
    h1_ref[...] = h1
    xe_ref[:, :d] = hn2
    xe_ref[:, d:] = coef
    if bk_ref is not None:
        bk_ref[...] = bucket


def _pool_windows(xp_ref, inv_cnt):
    m = xp_ref.shape[0] - HALO_POOL
    gd = xp_ref.shape[1] // len(POOL_WINDOWS)
    outs = []
    for gi, w in enumerate(POOL_WINDOWS):
        lanes = slice(gi * gd, (gi + 1) * gd)
        s = xp_ref[HALO_POOL:HALO_POOL + m, lanes]
        cur = s
        for j in range(1, w):
            s = s + xp_ref[HALO_POOL - j:HALO_POOL - j + m, lanes]
        outs.append(s * inv_cnt(w) - cur)
    return outs


def _mix_ab_kernel(hp_ref, osbp_ref, up_ref, uprev_ref, hs_ref, osbs_ref, us_ref, hist_ref,
                   wpool_ref, wpool_lo_ref, pscale_ref, wout_ref, wout_lo_ref, gf_ref, wr_hi_ref, wr_lo_ref, rb_ref,
                   h1p_ref, xep_ref, bkp_ref, h1s_ref, xes_ref, xp_ref, *, npt, tps, pos0):
    i = pl.program_id(0)
    d_sb = osbp_ref.shape[1]

    def tail(pooled, osb, h, precise_osb):
        o_pool = jnp.concatenate(
            [_dot3(*_split(pooled[gi]), wpool_ref[gi], wpool_lo_ref[gi]) for gi in range(len(POOL_WINDOWS))],
            axis=1) * pscale_ref[...]
        mix = (_mm(osb, wout_ref, wout_lo_ref, precise_osb, rows=slice(0, d_sb))
               + _mm(o_pool, wout_ref, wout_lo_ref, True, rows=slice(d_sb, None)))
        return h + mix

    tin = i % tps

    def prompt_tile(precise_osb):
        xp_ref[:HALO_POOL, :] = jnp.where(tin == 0, 0.0, uprev_ref[...])
        xp_ref[HALO_POOL:, :] = up_ref[...]
        pos = tin * TM + lax.broadcasted_iota(I32, (TM, 1), 0)

        def inv_cnt(w):
            return 1.0 / jnp.minimum(pos + 1, w).astype(F32)

        h1 = tail(_pool_windows(xp_ref, inv_cnt), osbp_ref[...], hp_ref[...], precise_osb)
        _ffn_norm_route_store(h1, gf_ref, wr_hi_ref, wr_lo_ref, rb_ref, h1p_ref, xep_ref, bkp_ref)

    @pl.when(jnp.logical_and(i < npt, tin == tps - 1))
    def _():
        prompt_tile(True)

    @pl.when(jnp.logical_and(i < npt, tin != tps - 1))
    def _():
        prompt_tile(False)

    @pl.when(i == npt)
    def _():
        gd = us_ref.shape[1] // len(POOL_WINDOWS)
        db = hist_ref.shape[1]
        u = us_ref[...]
        pooled = []
        for gi, w in enumerate(POOL_WINDOWS):
            lanes = slice(gi * gd, (gi + 1) * gd)
            s = u[:db, lanes]
            for j in range(1, w):
                s = s + hist_ref[POOL_HIST - j][:, lanes]
            s = jnp.concatenate([s, u[db:, lanes]], axis=0)
            pooled.append(s * (1.0 / min(pos0 + 1, w)) - u[:, lanes])
        h1 = tail(pooled, osbs_ref[...], hs_ref[...], True)
        _ffn_norm_route_store(h1, gf_ref, wr_hi_ref, wr_lo_ref, rb_ref, h1s_ref, xes_ref, None)


def _mix_ab(h_p, osb_p, u_p, h_s, osb_s, u_s, hist, wpool, wpool_lo, pscale, wout, wout_lo, gf, wr_hi, wr_lo, rb, *, seq, pos0):
    n, d = h_p.shape
    npt = n // TM
    tps = seq // TM
    d_sb = osb_p.shape[1]
    d_pool = u_p.shape[1]
    hb = TM // HALO_POOL
    consts = (wpool, wpool_lo, pscale, wout, wout_lo, gf, wr_hi, wr_lo, rb)
    in_specs = [
        _pspec(d, npt), _pspec(d_sb, npt), _pspec(d_pool, npt),
        pl.BlockSpec((HALO_POOL, d_pool), lambda i: (jnp.maximum(jnp.minimum(i, npt - 1) * hb - 1, 0), 0)),
        _const_spec((SP, d)), _const_spec((SP, d_sb)), _const_spec((SP, d_pool)), _const_spec(hist.shape),
    ] + [_const_spec(c.shape) for c in consts]
    de = d + LANES
    out_shape = (
        jax.ShapeDtypeStruct((n, d), F32), jax.ShapeDtypeStruct((n, de), F32), jax.ShapeDtypeStruct((1, n), I32),
        jax.ShapeDtypeStruct((SP, d), F32), jax.ShapeDtypeStruct((SP, de), F32),
    )
    out_specs = (
        _pspec(d, npt), _pspec(de, npt), pl.BlockSpec((1, TM), lambda i: (0, jnp.minimum(i, npt - 1))),
        _const_spec((SP, d)), _const_spec((SP, de)),
    )
    return pl.pallas_call(
        functools.partial(_mix_ab_kernel, npt=npt, tps=tps, pos0=pos0),
        grid=(npt + 1,),
        in_specs=in_specs, out_specs=out_specs, out_shape=out_shape,
        scratch_shapes=[pltpu.VMEM((HALO_POOL + TM, d_pool), F32)],
        compiler_params=_cparams(),
        name="mix_ab",
    )(h_p, osb_p, u_p, u_p, h_s, osb_s, u_s, hist, *consts)


def _sort_kernel(b_ref, pos_ref, tiles_ref, carry_ref, start_ref):
    ps = pl.program_id(0)
    j = pl.program_id(1)
    cw = b_ref.shape[1]
    rows = lax.broadcasted_iota(I32, (NB_ROWS, cw), 0)
    onehot = rows == b_ref[...]
    oh = jnp.where(onehot, 1.0, 0.0)
    counts = jnp.broadcast_to(jnp.sum(oh, axis=1, keepdims=True), (NB_ROWS, LANES))

    @pl.when(jnp.logical_and(ps == 0, j == 0))
    def _():
        carry_ref[...] = jnp.zeros_like(carry_ref)

    @pl.when(ps == 0)
    def _():
        carry_ref[...] += counts

    @pl.when(jnp.logical_and(ps == 1, j == 0))
    def _():
        real = lax.broadcasted_iota(I32, (NB_ROWS, LANES), 0) < N_BUCKETS
        tiles = jnp.where(real, jnp.floor((carry_ref[...] + (TMX - 1)) * (1.0 / TMX)), 0.0)
        rr = lax.broadcasted_iota(I32, (NB_ROWS, NB_ROWS), 0)
        cc = lax.broadcasted_iota(I32, (NB_ROWS, NB_ROWS), 1)
        below = jnp.where(cc < rr, 1.0, 0.0).astype(BF16)
        start_ref[...] = _dot(below, tiles.astype(BF16)) * TMX
        tiles_ref[...] = tiles
        carry_ref[...] = jnp.zeros_like(carry_ref)

    @pl.when(ps == 1)
    def _():
        rj = lax.broadcasted_iota(I32, (cw, cw), 0)
        ct = lax.broadcasted_iota(I32, (cw, cw), 1)
        before = jnp.where(rj < ct, 1.0, 0.0).astype(BF16)
        rank = _dot(oh.astype(BF16), before)
        val = rank + carry_ref[:, 0:1] + start_ref[:, 0:1]
        pos_ref[...] = jnp.sum(jnp.where(onehot, val, 0.0), axis=0, keepdims=True).astype(I32)
        carry_ref[...] += counts


def _sort_positions(buckets):
    npad = buckets.shape[1]
    nch = npad // SORT_CHUNK
    return pl.pallas_call(
        _sort_kernel,
        grid=(2, nch),
        in_specs=[pl.BlockSpec((1, SORT_CHUNK), lambda ps, j: (0, j))],
        out_specs=(pl.BlockSpec((1, SORT_CHUNK), lambda ps, j: (0, j * ps)),
                   pl.BlockSpec((NB_ROWS, LANES), lambda ps, j: (0, 0))),
        out_shape=(jax.ShapeDtypeStruct((1, npad), I32), jax.ShapeDtypeStruct((NB_ROWS, LANES), F32)),
        scratch_shapes=[pltpu.VMEM((NB_ROWS, LANES), F32), pltpu.VMEM((NB_ROWS, LANES), F32)],
        compiler_params=_cparams(2),
        name="moe_sort",
    )(buckets)


def _move_rows_kernel(pos_ref, *refs, scatter):
    if scatter:
        tok_ref, _, sorted_ref, sems = refs
    else:
        sorted_ref, tok_ref, sems = refs
    nb = tok_ref.shape[0] // MOVE_BATCH

    def copy(tok_rows, sorted_rows, slot):
        src, dst = (tok_ref.at[tok_rows], sorted_ref.at[sorted_rows])
        if not scatter:
            src, dst = dst, src
        return pltpu.make_async_copy(src, dst, sems.at[slot])

    def batch_wait(slot):
        copy(pl.ds(0, MOVE_BATCH), pl.ds(0, MOVE_BATCH), slot).wait()

    def batch(b, _):
        slot = lax.rem(b, MOVE_SEMS)

        @pl.when(b >= MOVE_SEMS)
        def _():
            batch_wait(slot)

        def row(r, _):
            t = b * MOVE_BATCH + r
            copy(pl.ds(t, 1), pl.ds(pos_ref[0, t], 1), slot).start()
            return 0

        lax.fori_loop(0, MOVE_BATCH, row, 0, unroll=8)
        return 0

    lax.fori_loop(0, nb, batch, 0)

    def drain(b, _):
        batch_wait(lax.rem(b, MOVE_SEMS))
        return 0

    lax.fori_loop(max(nb - MOVE_SEMS, 0), nb, drain, 0)


def _scatter_rows(pos, src, dst):
    n, cols = src.shape
    chunk = min(MOVE_CHUNK, n)
    assert n % chunk == 0 and chunk % MOVE_BATCH == 0
    return pl.pallas_call(
        functools.partial(_move_rows_kernel, scatter=True),
        grid=(n // chunk,),
        in_specs=[pl.BlockSpec((1, chunk), lambda i: (0, i), memory_space=pltpu.SMEM),
                  pl.BlockSpec((chunk, cols), lambda i: (i, 0)),
                  pl.BlockSpec(memory_space=pl.ANY)],
        out_specs=pl.BlockSpec(memory_space=pl.ANY),
        out_shape=jax.ShapeDtypeStruct(dst.shape, dst.dtype),
        scratch_shapes=[pltpu.SemaphoreType.DMA((MOVE_SEMS,))],
        input_output_aliases={2: 0},
        compiler_params=_cparams(),
        name="moe_scatter",
    )(pos, src, dst)


def _gather_rows(pos, src):
    n = pos.shape[1]
    cols = src.shape[1]
    chunk = min(MOVE_CHUNK, n)
    assert n % chunk == 0 and chunk % MOVE_BATCH == 0
    return pl.pallas_call(
        functools.partial(_move_rows_kernel, scatter=False),
        grid=(n // chunk,),
        in_specs=[pl.BlockSpec((1, chunk), lambda i: (0, i), memory_space=pltpu.SMEM),
                  pl.BlockSpec(memory_space=pl.ANY)],
        out_specs=pl.BlockSpec((chunk, cols), lambda i: (i, 0)),
        out_shape=jax.ShapeDtypeStruct((n, cols), src.dtype),
        scratch_shapes=[pltpu.SemaphoreType.DMA((MOVE_SEMS,))],
        compiler_params=_cparams(),
        name="moe_gather",
    )(pos, src)


def _experts_kernel(ea_ref, eb_ref, valid_ref, xs_ref, wga_ref, wua_ref, wda_ref, wgb_ref, wub_ref, wdb_ref, ys_ref):
    i = pl.program_id(0)
    d = ys_ref.shape[1]

    @pl.when(valid_ref[i] == 1)
    def _():
        x = xs_ref[:, :d].astype(BF16)

        def expert(wg_ref, wu_ref, wd_ref):
            g = _dot(x, wg_ref[0].astype(BF16))
            u = _dot(x, wu_ref[0].astype(BF16))
            hid = g * _sigmoid(g) * u
            return _dot(hid.astype(BF16), wd_ref[0].astype(BF16))

        ya = expert(wga_ref, wua_ref, wda_ref)
        yb = expert(wgb_ref, wub_ref, wdb_ref)
        ys_ref[...] = xs_ref[:, d:d + 1] * ya + xs_ref[:, d + 1:d + 2] * yb

    @pl.when(valid_ref[i] == 0)
    def _():
        ys_ref[...] = jnp.zeros_like(ys_ref)


def _experts(ea, eb, valid, xs, w_gate, w_up, w_down):
    ns, de = xs.shape
    d = de - LANES
    dx = w_gate.shape[-1]
    ntiles = ns // TMX

    def wspec(shape, which):
        return pl.BlockSpec((1,) + shape, lambda i, ea, eb, valid: ((ea, eb)[which][i], 0, 0))

    grid_spec = pltpu.PrefetchScalarGridSpec(
        num_scalar_prefetch=3,
        grid=(ntiles,),
        in_specs=[pl.BlockSpec((TMX, de), lambda i, ea, eb, valid: (i, 0)),
                  wspec((d, dx), 0), wspec((d, dx), 0), wspec((dx, d), 0),
                  wspec((d, dx), 1), wspec((d, dx), 1), wspec((dx, d), 1)],
        out_specs=pl.BlockSpec((TMX, d), lambda i, ea, eb, valid: (i, 0)),
    )
    return pl.pallas_call(
        _experts_kernel,
        grid_spec=grid_spec,
        out_shape=jax.ShapeDtypeStruct((ns, d), F32),
        compiler_params=_cparams(),
        name="moe_experts",
    )(ea, eb, valid, xs, w_gate, w_up, w_down, w_gate, w_up, w_down)


def _moe_sparse(xe, bk, w_gate, w_up, w_down, layer):
    n_tok, de = xe.shape
    npad = -(-n_tok // SORT_CHUNK) * SORT_CHUNK
    buckets = jnp.concatenate([bk, jnp.full((1, npad - n_tok), N_BUCKETS, I32)], axis=1)
    pos, tiles = _sort_positions(buckets)

    ntiles = -(-n_tok // TMX) + N_BUCKETS
    tiles_per_bucket = tiles[:N_BUCKETS, 0].astype(I32)
    cum = jnp.cumsum(tiles_per_bucket)
    tile_ids = jnp.arange(ntiles, dtype=I32)
    valid = tile_ids < cum[-1]
    tb = jnp.sum((jnp.minimum(tile_ids, cum[-1] - 1)[:, None] >= cum[None, :]).astype(I32), axis=1)
    group, pidx = tb // len(PAIRS), tb % len(PAIRS)
    pa = jnp.asarray([p[0] for p in PAIRS], I32)[pidx]
    pb = jnp.asarray([p[1] for p in PAIRS], I32)[pidx]
    first = layer * N_EXPERTS + group * EXPERTS_PER_GROUP
    flat = lambda w: w.reshape((-1,) + w.shape[2:])

    pos = pos[:, :n_tok]
    xs = _scatter_rows(pos, xe, jnp.zeros((ntiles * TMX, de), F32))
    ys = _experts(first + pa, first + pb, valid.astype(I32), xs, flat(w_gate), flat(w_up), flat(w_down))
    return _gather_rows(pos, ys)


def _experts_dense_kernel(xe_ref, wg_ref, wu_ref, wd_ref, y_ref):
    e = pl.program_id(0)
    d = y_ref.shape[1]

    @pl.when(e == 0)
    def _():
        y_ref[...] = jnp.zeros_like(y_ref)

    x_hi, x_lo = _split(xe_ref[:, :d])
    g = _dot3(x_hi, x_lo, *_split(wg_ref[0, 0]))
    u = _dot3(x_hi, x_lo, *_split(wu_ref[0, 0]))
    hid = g * _sigmoid(g) * u
    y = _dot3(*_split(hid), *_split(wd_ref[0, 0]))
    lane = lax.broadcasted_iota(I32, (xe_ref.shape[0], LANES), 1)
    coef = jnp.sum(jnp.where(lane == e, xe_ref[:, d:], 0.0), axis=1, keepdims=True)
    y_ref[...] += coef * y


def _moe_dense(xe, w_gate, w_up, w_down, layer):
    m, de = xe.shape
    d = de - LANES
    _, n_exp, _, dx = w_gate.shape
    return pl.pallas_call(
        _experts_dense_kernel,
        grid=(n_exp,),
        in_specs=[_const_spec((m, de)),
                  pl.BlockSpec((1, 1, d, dx), lambda e: (layer, e, 0, 0)), pl.BlockSpec((1, 1, d, dx), lambda e: (layer, e, 0, 0)),
                  pl.BlockSpec((1, 1, dx, d), lambda e: (layer, e, 0, 0))],
        out_specs=_const_spec((m, d)),
        out_shape=jax.ShapeDtypeStruct((m, d), F32),
        compiler_params=_cparams(),
        name="moe_dense",
    )(xe, w_gate, w_up, w_down)


def _ple(h, y, p, gpg_ref, wpg_ref, wpg_lo_ref, bpg_ref, wple_ref, wple_lo_ref, precise):
    h2 = h + y
    gate = _sigmoid(_mm(_rms(h2, gpg_ref[...]), wpg_ref, wpg_lo_ref, precise) + bpg_ref[...])
    return h2 + gate * _mm(p, wple_ref, wple_lo_ref, precise)


def _ple_inproj_c_kernel(hp_ref, yp_ref, pp_ref, hs_ref, ys_ref, ps_ref,
                         gpg_ref, wpg_ref, wpg_lo_ref, bpg_ref, wple_ref, wple_lo_ref, gm_ref, win_ref, win_lo_ref,
                         h3p_ref, ucp_ref, h3s_ref, ucs_ref, *, npt):
    i = pl.program_id(0)

    def run(h_ref, y_ref, p_ref, h3_ref, uc_ref, precise):
        h3 = _ple(h_ref[...], y_ref[...], p_ref[...], gpg_ref, wpg_ref, wpg_lo_ref, bpg_ref, wple_ref, wple_lo_ref, precise)
        h3_ref[...] = h3
        ag = _mm(_rms(h3, gm_ref[...]), win_ref, win_lo_ref, precise)
        dc = ag.shape[1] // 2
        uc_ref[...] = ag[:, :dc] * _sigmoid(ag[:, dc:])

    @pl.when(i < npt)
    def _():
        run(hp_ref, yp_ref, pp_ref, h3p_ref, ucp_ref, False)

    @pl.when(i == npt)
    def _():
        run(hs_ref, ys_ref, ps_ref, h3s_ref, ucs_ref, True)


def _ple_inproj_c(h_p, y_p, p_all, layer, h_s, y_s, p_s, gpg, wpg, wpg_lo, bpg, wple, wple_lo, gm, win, win_lo):
    n, d = h_p.shape
    npt = n // TM
    dple = p_all.shape[1]
    dc = win.shape[1] // 2
    consts = (gpg, wpg, wpg_lo, bpg, wple, wple_lo, gm, win, win_lo)
    return pl.pallas_call(
        functools.partial(_ple_inproj_c_kernel, npt=npt),
        grid=(npt + 1,),
        in_specs=[_pspec(d, npt), _pspec(d, npt), _pspec(dple, npt, first_block=layer * npt),
                  _const_spec((SP, d)), _const_spec((SP, d)), _const_spec((SP, dple))]
        + [_const_spec(c.shape) for c in consts],
        out_specs=(_pspec(d, npt), _pspec(dc, npt), _const_spec((SP, d)), _const_spec((SP, dc))),
        out_shape=(jax.ShapeDtypeStruct((n, d), F32), jax.ShapeDtypeStruct((n, dc), F32),
                   jax.ShapeDtypeStruct((SP, d), F32), jax.ShapeDtypeStruct((SP, dc), F32)),
        compiler_params=_cparams(),
        name="ple_inproj_c",
    )(h_p, y_p, p_all, h_s, y_s, p_s, *consts)


def _ple_final_kernel(hp_ref, yp_ref, pp_ref, hs_ref, ys_ref, ps_ref,
                      gpg_ref, wpg_ref, wpg_lo_ref, bpg_ref, wple_ref, wple_lo_ref, gfin_ref, op_ref, os_ref, *, npt):
    i = pl.program_id(0)

    def run(h_ref, y_ref, p_ref, o_ref, precise):
        h = _ple(h_ref[...], y_ref[...], p_ref[...], gpg_ref, wpg_ref, wpg_lo_ref, bpg_ref, wple_ref, wple_lo_ref, precise)
        o_ref[...] = _rms(h, gfin_ref[...])

    @pl.when(i < npt)
    def _():
        run(hp_ref, yp_ref, pp_ref, op_ref, False)

    @pl.when(i == npt)
    def _():
        run(hs_ref, ys_ref, ps_ref, os_ref, True)


def _ple_final(h_p, y_p, p_all, layer, h_s, y_s, p_s, gpg, wpg, wpg_lo, bpg, wple, wple_lo, gfin):
    n, d = h_p.shape
    npt = n // TM
    dple = p_all.shape[1]
    consts = (gpg, wpg, wpg_lo, bpg, wple, wple_lo, gfin)
    return pl.pallas_call(
        functools.partial(_ple_final_kernel, npt=npt),
        grid=(npt + 1,),
        in_specs=[_pspec(d, npt), _pspec(d, npt), _pspec(dple, npt, first_block=layer * npt),
                  _const_spec((SP, d)), _const_spec((SP, d)), _const_spec((SP, dple))]
        + [_const_spec(c.shape) for c in consts],
        out_specs=(_pspec(d, npt), _const_spec((SP, d))),
        out_shape=(jax.ShapeDtypeStruct((n, d), F32), jax.ShapeDtypeStruct((SP, d), F32)),
        compiler_params=_cparams(),
        name="ple_final",
    )(h_p, y_p, p_all, h_s, y_s, p_s, *consts)


def _conv_tail(y, h, bdw_ref, lng_ref, lnb_ref, wout_ref, wout_lo_ref, precise):
    y = y + bdw_ref[...]
    mu = jnp.mean(y, axis=-1, keepdims=True)
    yc = y - mu
    yn = yc * lax.rsqrt(jnp.mean(yc * yc, axis=-1, keepdims=True) + EPS) * lng_ref[...] + lnb_ref[...]
    act = yn * _sigmoid(yn)
    return h + _mm(act, wout_ref, wout_lo_ref, precise)


def _mix_c_kernel(hp_ref, ucp_ref, ucprev_ref, hs_ref, ucs_ref, hist_ref,
                  wdw_ref, bdw_ref, lng_ref, lnb_ref, wout_ref, wout_lo_ref, gf_ref, wr_hi_ref, wr_lo_ref, rb_ref,
                  h1p_ref, xep_ref, bkp_ref, h1s_ref, xes_ref, xp_ref, sh_ref, wb_ref, y_ref, *, npt, tps):
    i = pl.program_id(0)

    @pl.when(i < npt)
    def _():
        tin = i % tps
        xp_ref[:HALO_CONV, :] = jnp.where(tin == 0, 0.0, ucprev_ref[...])
        xp_ref[HALO_CONV:, :] = ucp_ref[...]
        off = HALO_CONV - CONV_HIST
        half = TM // 2
        ext = sh_ref.shape[1]
        groups = CONV_ROWS // CONV_SUBLANES
        for j in range(CONV_WIDTH):
            wb_ref[j] = jnp.broadcast_to(wdw_ref[j:j + 1, :], wb_ref.shape[1:])
        for hb in range(TM // half):
            base = hb * half
            for s in range(1, CONV_SUBLANES):
                sh_ref[s - 1] = xp_ref[base + s:base + s + ext, :]

            def chunk(c, _):
                r0 = pl.multiple_of(c * CONV_ROWS, CONV_ROWS)
                accs = [None] * groups
                for j in range(CONV_WIDTH):
                    s, k8 = (off + j) % CONV_SUBLANES, (off + j) // CONV_SUBLANES * CONV_SUBLANES
                    wb = wb_ref[j]
                    for g in range(groups):
                        rows = pl.ds(r0 + k8 + g * CONV_SUBLANES, CONV_SUBLANES)
                        x = xp_ref[pl.ds(base + r0 + k8 + g * CONV_SUBLANES, CONV_SUBLANES), :] if s == 0 else sh_ref[s - 1, rows, :]
                        accs[g] = wb * x if accs[g] is None else accs[g] + wb * x
                for g in range(groups):
                    y_ref[pl.ds(base + r0 + g * CONV_SUBLANES, CONV_SUBLANES), :] = accs[g]
                return 0

            lax.fori_loop(0, half // CONV_ROWS, chunk, 0)
        h1 = _conv_tail(y_ref[...], hp_ref[...], bdw_ref, lng_ref, lnb_ref, wout_ref, wout_lo_ref, False)
        _ffn_norm_route_store(h1, gf_ref, wr_hi_ref, wr_lo_ref, rb_ref, h1p_ref, xep_ref, bkp_ref)

    @pl.when(i == npt)
    def _():
        db = hist_ref.shape[1]
        y = wdw_ref[CONV_WIDTH - 1:CONV_WIDTH, :] * ucs_ref[...]
        y_hist = wdw_ref[0:1, :] * hist_ref[0]
        for j in range(1, CONV_HIST):
            y_hist = y_hist + wdw_ref[j:j + 1, :] * hist_ref[j]
        y = jnp.concatenate([y[:db] + y_hist, y[db:]], axis=0)
        h1 = _conv_tail(y, hs_ref[...], bdw_ref, lng_ref, lnb_ref, wout_ref, wout_lo_ref, True)
        _ffn_norm_route_store(h1, gf_ref, wr_hi_ref, wr_lo_ref, rb_ref, h1s_ref, xes_ref, None)


def _mix_c(h_p, uc_p, h_s, uc_s, hist, wdw, bdw, lng, lnb, wout, wout_lo, gf, wr_hi, wr_lo, rb, *, seq):
    n, d = h_p.shape
    npt = n // TM
    tps = seq // TM
    dc = uc_p.shape[1]
    hb = TM // HALO_CONV
    shifted_reach = (HALO_CONV - CONV_HIST + CONV_WIDTH - 2) // CONV_SUBLANES * CONV_SUBLANES
    assert (TM // 2) % CONV_ROWS == 0 and CONV_ROWS % CONV_SUBLANES == 0
    consts = (wdw, bdw, lng, lnb, wout, wout_lo, gf, wr_hi, wr_lo, rb)
    in_specs = [
        _pspec(d, npt), _pspec(dc, npt),
        pl.BlockSpec((HALO_CONV, dc), lambda i: (jnp.maximum(jnp.minimum(i, npt - 1) * hb - 1, 0), 0)),
        _const_spec((SP, d)), _const_spec((SP, dc)), _const_spec(hist.shape),
    ] + [_const_spec(c.shape) for c in consts]
    de = d + LANES
    out_shape = (
        jax.ShapeDtypeStruct((n, d), F32), jax.ShapeDtypeStruct((n, de), F32), jax.ShapeDtypeStruct((1, n), I32),
        jax.ShapeDtypeStruct((SP, d), F32), jax.ShapeDtypeStruct((SP, de), F32),
    )
    out_specs = (
        _pspec(d, npt), _pspec(de, npt), pl.BlockSpec((1, TM), lambda i: (0, jnp.minimum(i, npt - 1))),
        _const_spec((SP, d)), _const_spec((SP, de)),
    )
    return pl.pallas_call(
        functools.partial(_mix_c_kernel, npt=npt, tps=tps),
        grid=(npt + 1,),
        in_specs=in_specs, out_specs=out_specs, out_shape=out_shape,
        scratch_shapes=[pltpu.VMEM((HALO_CONV + TM, dc), F32),
                        pltpu.VMEM((CONV_SUBLANES - 1, TM // 2 + shifted_reach, dc), F32),
                        pltpu.VMEM((CONV_WIDTH, CONV_SUBLANES, dc), F32),
                        pltpu.VMEM((TM, dc), F32)],
        compiler_params=_cparams(),
        name="mix_c",
    )(h_p, uc_p, uc_p, h_s, uc_s, hist, *consts)


def _router_weights(w_rg, b_rg, w_re, b_re):
    d = w_rg.shape[0]
    wt = jnp.zeros((ROUTE_ROWS, d), F32).at[:N_EXPERT_GROUPS].set(w_rg.T).at[8:8 + w_re.shape[1]].set(w_re.T)
    hi, lo = _hi_lo(wt)
    rb = jnp.zeros((ROUTE_ROWS, 1), F32).at[:N_EXPERT_GROUPS, 0].set(b_rg).at[8:8 + w_re.shape[1], 0].set(b_re)
    return hi, lo, rb


def _pad_rows(x, rows):
    return jnp.concatenate([x, jnp.zeros((rows - x.shape[0],) + x.shape[1:], x.dtype)], axis=0)


def kernel(x_prompt, x_sample, p_prompt, p_sample, cache_k, cache_v, state_pool, state_conv, page_table, norm_mix, norm_ffn, norm_final, w_in_ab, sb_bias, w_pool, pool_scale, w_out_ab, w_in_c, w_dw, b_dw, ln_g_c, ln_b_c, w_out_c, w_rg, b_rg, w_re, b_re, w_gate_e, w_up_e, w_down_e, w_ple, w_ple_gate, b_ple_gate, g_ple_gate):
    batch, seq, d = x_prompt.shape
    db, dec_seq, _ = x_sample.shape
    depth = norm_mix.shape[0]
    d_sb = SB_HEADS * SB_HEAD_DIM
    n_pages = page_table.shape[1]
    page = cache_k.shape[2]
    pos0 = n_pages * page
    assert dec_seq == 1 and depth == 2 and db <= SP and db % 8 == 0 and w_gate_e.shape[1] == N_EXPERTS
    assert seq % TM == 0 and seq % TQ == 0 and TMX & (TMX - 1) == 0
    assert max(CONV_HIST, POOL_HIST) <= TQ <= TM
    n_p = batch * seq

    row = lambda v: v.reshape(1, -1)
    x_p = x_prompt.reshape(n_p, d)
    x_s = _pad_rows(x_sample.reshape(db, d), SP)
    p_all = p_prompt.reshape(depth * n_p, -1)
    ps = [_pad_rows(p_sample[i].reshape(db, -1), SP) for i in range(depth)]

    w_in, w_in_lo = _hi_lo(w_in_ab[0])
    w_kv_t, w_kv_t_lo = _hi_lo(w_in_ab[0][:, d_sb:3 * d_sb].T)
    qh_p, ql_p, vh_p, vl_p, u_p, kt_p, vt_p, kth_p, ktl_p, q_s, u_s, k_s, v_s = _inproj_ab(
        x_p, x_s, row(norm_mix[0]), w_in, w_in_lo, w_kv_t, w_kv_t_lo[:d_sb], batch=batch, seq=seq)
    osb_p = _sb_prompt(sb_bias[0], qh_p, ql_p, kth_p, ktl_p, vh_p, vl_p, batch=batch, seq=seq)
    kt_pages = jnp.transpose(cache_k[0], (0, 2, 3, 1))
    vt_pages = jnp.transpose(cache_v[0], (0, 2, 3, 1))
    bias_b = jnp.broadcast_to(sb_bias[0][:, None], (SB_HEADS, page))
    osb_s = _sb_sample(page_table, q_s[:db].reshape(db, 1, d_sb), bias_b, kt_pages, vt_pages)
    osb_s = _pad_rows(osb_s.reshape(db, d_sb), SP)

    pool_hist = jnp.transpose(state_pool[0], (1, 0, 2))
    wr_hi, wr_lo, rb = _router_weights(w_rg[0], b_rg[0], w_re[0], b_re[0])
    w_pool_hi, w_pool_lo = _hi_lo(w_pool[0])
    w_out_hi, w_out_lo = _hi_lo(w_out_ab[0])
    h1_p, xe_p, bk_p, h1_s, xe_s = _mix_ab(
        x_p, osb_p, u_p, x_s, osb_s, u_s, pool_hist, w_pool_hi, w_pool_lo, row(pool_scale[0]),
        w_out_hi, w_out_lo, row(norm_ffn[0]), wr_hi, wr_lo, rb, seq=seq, pos0=pos0)
    y_p = _moe_sparse(xe_p, bk_p, w_gate_e, w_up_e, w_down_e, 0)
    y_s = _moe_dense(xe_s, w_gate_e, w_up_e, w_down_e, 0)

    w_pg, w_pg_lo = _hi_lo(w_ple_gate[0])
    w_pl, w_pl_lo = _hi_lo(w_ple[0])
    w_ic, w_ic_lo = _hi_lo(w_in_c[0])
    h3_p, uc_p, h3_s, uc_s = _ple_inproj_c(
        h1_p, y_p, p_all, 0, h1_s, y_s, ps[0], row(g_ple_gate[0]), w_pg, w_pg_lo, row(b_ple_gate[0]),
        w_pl, w_pl_lo, row(norm_mix[1]), w_ic, w_ic_lo)
    conv_hist = jnp.transpose(state_conv[0], (1, 0, 2))
    wr_hi, wr_lo, rb = _router_weights(w_rg[1], b_rg[1], w_re[1], b_re[1])
    w_oc, w_oc_lo = _hi_lo(w_out_c[0])
    h4_p, xe_p, bk_p, h4_s, xe_s = _mix_c(
        h3_p, uc_p, h3_s, uc_s, conv_hist, w_dw[0], row(b_dw[0]), row(ln_g_c[0]), row(ln_b_c[0]),
        w_oc, w_oc_lo, row(norm_ffn[1]), wr_hi, wr_lo, rb, seq=seq)
    y_p = _moe_sparse(xe_p, bk_p, w_gate_e, w_up_e, w_down_e, 1)
    y_s = _moe_dense(xe_s, w_gate_e, w_up_e, w_down_e, 1)
    w_pg, w_pg_lo = _hi_lo(w_ple_gate[1])
    w_pl, w_pl_lo = _hi_lo(w_ple[1])
    out_p, out_s = _ple_final(
        h4_p, y_p, p_all, 1, h4_s, y_s, ps[1], row(g_ple_gate[1]), w_pg, w_pg_lo, row(b_ple_gate[1]),
        w_pl, w_pl_lo, row(norm_final))

    def heads_last(t):
        return jnp.transpose(t.reshape(batch, SB_HEADS, SB_HEAD_DIM, seq), (0, 3, 1, 2))[None]

    y_prompt = out_p.reshape(batch, seq, d)
    y_sample = out_s[:db].reshape(db, 1, d)
    k_prompt, v_prompt = heads_last(kt_p), heads_last(vt_p)
    pool_prompt = u_p.reshape(batch, seq, -1)[:, seq - POOL_HIST:][None]
    conv_prompt = uc_p.reshape(batch, seq, -1)[:, seq - CONV_HIST:][None]
    k_sample = k_s[:db].reshape(1, db, 1, SB_HEADS, SB_HEAD_DIM)
    v_sample = v_s[:db].reshape(1, db, 1, SB_HEADS, SB_HEAD_DIM)
    pool_sample = jnp.transpose(jnp.concatenate([pool_hist[1:], u_s[None, :db]], axis=0), (1, 0, 2))[None]
    conv_sample = jnp.transpose(jnp.concatenate([conv_hist[1:], uc_s[None, :db]], axis=0), (1, 0, 2))[None]
    return (y_prompt, y_sample, k_prompt, v_prompt, pool_prompt, conv_prompt, k_sample, v_sample, pool_sample, conv_sample)
```

```python
import functools

import jax
import jax.numpy as jnp
from jax import lax
from jax.experimental import pallas as pl
from jax.experimental.pallas import tpu as pltpu

F32 = jnp.float32
BF16 = jnp.bfloat16
I32 = jnp.int32

EPS = 1e-6
SB_HEADS = 8
SB_HEAD_DIM = 64
POOL_WINDOWS = (2, 4, 8, 16)
POOL_HIST = max(POOL_WINDOWS) - 1
CONV_WIDTH = 31
CONV_HIST = CONV_WIDTH - 1
N_EXPERT_GROUPS = 4
EXPERTS_PER_GROUP = 4
N_EXPERTS = N_EXPERT_GROUPS * EXPERTS_PER_GROUP
PAIRS = ((0, 1), (0, 2), (0, 3), (1, 3), (1, 2), (2, 3))
N_BUCKETS = N_EXPERT_GROUPS * len(PAIRS)

LANES = 128
TM = 512
SP = 128
TQ = 256
TMX = 256
SORT_CHUNK = 512
NB_ROWS = 32
ROUTE_ROWS = 32
HALO_POOL = 16
HALO_CONV = 32
CONV_SUBLANES = 8
CONV_ROWS = 32
PAGES_PER_STEP = 16
MOVE_CHUNK = 2048
MOVE_BATCH = 128
MOVE_SEMS = 8
NEG_BIG = -1e30
LOG2E = 1.4426950408889634
SPLIT_FACTOR = 65537.0
VMEM_LIMIT = 56 * 1024 * 1024


def _cparams(n_axes=1, vmem=VMEM_LIMIT):
    return pltpu.CompilerParams(dimension_semantics=("arbitrary",) * n_axes, vmem_limit_bytes=vmem)


def _rms(x, g):
    return x * lax.rsqrt(jnp.mean(x * x, axis=-1, keepdims=True) + EPS) * g


def _sigmoid(x):
    return 1.0 / (1.0 + jnp.exp(-x))


def _softplus(z):
    return jnp.maximum(z, 0.0) + jnp.log(1.0 + jnp.exp(-jnp.abs(z)))


def _dot(a, b):
    return jnp.dot(a, b, preferred_element_type=F32)


def _dot_nt(a, b):
    return lax.dot_general(a, b, (((1,), (1,)), ((), ())), preferred_element_type=F32)


def _split(a):
    c = a * SPLIT_FACTOR
    hi = c - (c - a)
    return hi.astype(BF16), (a - hi).astype(BF16)


def _dot3(a_hi, a_lo, w_hi, w_lo):
    return _dot(a_hi, w_hi) + (_dot(a_hi, w_lo) + _dot(a_lo, w_hi))


def _mm(a, w_ref, wlo_ref, precise, rows=slice(None), cols=slice(None)):
    if precise:
        return _dot3(*_split(a), w_ref[rows, cols], wlo_ref[rows, cols])
    return _dot(a.astype(BF16), w_ref[rows, cols])


def _hi_lo(w):
    return _split(w)


def _pspec(cols, npt, rows=TM, first_block=0):
    return pl.BlockSpec((rows, cols), lambda i: (first_block + jnp.minimum(i, npt - 1), 0))


def _const_spec(shape):
    nd = len(shape)
    return pl.BlockSpec(shape, lambda i: (0,) * nd)


def _inproj_ab_kernel(xp_ref, xs_ref, g_ref, w_ref, wlo_ref, wt_ref, wtlo_ref,
                      qhp_ref, qlp_ref, vhp_ref, vlp_ref, up_ref, ktp_ref, vtp_ref, khtp_ref, kltp_ref,
                      qs_ref, us_ref, ks_ref, vs_ref, *, npt, tps, d_sb):
    i = pl.program_id(0)
    ucols = slice(3 * d_sb, None)
    last_in_seq = i % tps == tps - 1

    def prompt_tile(precise_q):
        hn, hn_lo = _split(_rms(xp_ref[...], g_ref[...]))

        def proj(cols):
            return _dot3(hn, hn_lo, w_ref[:, cols], wlo_ref[:, cols])

        qcols = slice(0, d_sb)
        q = (proj(qcols) if precise_q else _dot(hn, w_ref[:, qcols])) * (SB_HEAD_DIM ** -0.5)
        qhp_ref[...], qlp_ref[...] = _split(q)
        vu = proj(slice(2 * d_sb, None))
        vhp_ref[...], vlp_ref[...] = _split(vu[:, :d_sb])
        up_ref[...] = vu[:, d_sb:]
        kt = _dot_nt(wt_ref[:d_sb], hn) + (_dot_nt(wtlo_ref[...], hn) + _dot_nt(wt_ref[:d_sb], hn_lo))
        ktp_ref[0] = kt
        khtp_ref[0], kltp_ref[0] = _split(kt)
        vtp_ref[0] = _dot_nt(wt_ref[d_sb:], hn)

    @pl.when(jnp.logical_and(i < npt, last_in_seq))
    def _():
        prompt_tile(True)

    @pl.when(jnp.logical_and(i < npt, jnp.logical_not(last_in_seq)))
    def _():
        prompt_tile(False)

    @pl.when(i == npt)
    def _():
        pr = _mm(_rms(xs_ref[...], g_ref[...]), w_ref, wlo_ref, True)
        qs_ref[...] = pr[:, :d_sb] * (SB_HEAD_DIM ** -0.5)
        ks_ref[...] = pr[:, d_sb:2 * d_sb]
        vs_ref[...] = pr[:, 2 * d_sb:3 * d_sb]
        us_ref[...] = pr[:, ucols]


def _inproj_ab(x_p, x_s, g, w, w_lo, wt, wt_lo, *, batch, seq):
    n, d = x_p.shape
    npt = n // TM
    tps = seq // TM
    d_sb = SB_HEADS * SB_HEAD_DIM
    d_pool = w.shape[1] - 3 * d_sb
    tspec = pl.BlockSpec((1, d_sb, TM), lambda i: (jnp.minimum(i, npt - 1) // tps, 0, jnp.minimum(i, npt - 1) % tps))
    out_shape = (
        jax.ShapeDtypeStruct((n, d_sb), BF16),
        jax.ShapeDtypeStruct((n, d_sb), BF16),
        jax.ShapeDtypeStruct((n, d_sb), BF16),
        jax.ShapeDtypeStruct((n, d_sb), BF16),
        jax.ShapeDtypeStruct((n, d_pool), F32),
        jax.ShapeDtypeStruct((batch, d_sb, seq), F32),
        jax.ShapeDtypeStruct((batch, d_sb, seq), F32),
        jax.ShapeDtypeStruct((batch, d_sb, seq), BF16),
        jax.ShapeDtypeStruct((batch, d_sb, seq), BF16),
        jax.ShapeDtypeStruct((SP, d_sb), F32),
        jax.ShapeDtypeStruct((SP, d_pool), F32),
        jax.ShapeDtypeStruct((SP, d_sb), F32),
        jax.ShapeDtypeStruct((SP, d_sb), F32),
    )
    out_specs = (
        _pspec(d_sb, npt), _pspec(d_sb, npt), _pspec(d_sb, npt), _pspec(d_sb, npt), _pspec(d_pool, npt),
        tspec, tspec, tspec, tspec,
        _const_spec((SP, d_sb)), _const_spec((SP, d_pool)), _const_spec((SP, d_sb)), _const_spec((SP, d_sb)),
    )
    consts = (g, w, w_lo, wt, wt_lo)
    return pl.pallas_call(
        functools.partial(_inproj_ab_kernel, npt=npt, tps=tps, d_sb=d_sb),
        grid=(npt + 1,),
        in_specs=[_pspec(d, npt), _const_spec((SP, d))] + [_const_spec(c.shape) for c in consts],
        out_specs=out_specs,
        out_shape=out_shape,
        compiler_params=_cparams(),
        name="inproj_ab",
    )(x_p, x_s, *consts)


def _sb_prompt_kernel(bias_ref, qh_ref, ql_ref, kth_ref, ktl_ref, vh_ref, vl_ref, o_ref, carry_ref, acc_ref):
    qi = pl.program_id(1)
    nq = pl.num_programs(1)
    tq = qh_ref.shape[0]
    rk = lax.broadcasted_iota(I32, (tq, tq), 0)
    ck = lax.broadcasted_iota(I32, (tq, tq), 1)
    tri = jnp.where(rk >= ck, 1.0, 0.0).astype(BF16)

    carry_ref[...] = jnp.zeros_like(carry_ref)
    acc_ref[...] = jnp.zeros_like(acc_ref)

    low_half = lax.broadcasted_iota(I32, (tq, LANES), 1) < SB_HEAD_DIM
    causal = ck < rk

    def block(kj, diagonal, precise):
        ks = kj * tq if isinstance(kj, int) else pl.multiple_of(kj * tq, tq)
        for h in range(SB_HEADS):
            p, half = divmod(h, 2)
            lanes = slice(p * LANES, (p + 1) * LANES)
            own = low_half if half == 0 else jnp.logical_not(low_half)

            def head_q(ref):
                q_pair = ref[:, lanes]
                return jnp.where(own, q_pair, jnp.zeros_like(q_pair))

            q_h = head_q(qh_ref)
            kt = kth_ref[0, lanes, pl.ds(ks, tq)]
            z = _dot(q_h, kt)
            if precise:
                z = z + (_dot(q_h, ktl_ref[0, lanes, pl.ds(ks, tq)]) + _dot(head_q(ql_ref), kt))
            z = z + bias_ref[h]
            sp = jnp.maximum(z, 0.0) + jnp.log(1.0 + jnp.exp2(jnp.abs(z) * (-LOG2E)))
            if diagonal:
                sp = jnp.where(causal, sp, 0.0)
            carry = carry_ref[h]
            cum = _dot(sp.astype(BF16), tri)
            arg = z - cum
            if diagonal:
                arg = jnp.where(causal, arg, NEG_BIG)
            w = jnp.exp(arg)
            v = vh_ref[pl.ds(ks, tq), lanes]
            if precise:
                w_hi, w_lo = _split(w)
                pv = _dot(w_hi, v) + (_dot(w_hi, vl_ref[pl.ds(ks, tq), lanes]) + _dot(w_lo, v))
            else:
                pv = _dot(w.astype(BF16), v)
            acc_ref[h] += jnp.exp(carry) * pv
            carry_ref[h] = carry - jnp.broadcast_to(jnp.sum(sp, axis=1, keepdims=True), carry.shape)

    def sweep(precise):
        block(qi, True, precise)
        if precise:
            def body(j, _):
                block(qi - 1 - j, False, precise)
                return 0

            lax.fori_loop(0, qi, body, 0)
            return

        def body2(j, _):
            block(qi - 1 - 2 * j, False, precise)
            block(qi - 2 - 2 * j, False, precise)
            return 0

        lax.fori_loop(0, qi // 2, body2, 0)

        @pl.when(qi % 2 == 1)
        def _():
            block(0, False, precise)

    @pl.when(qi < nq - 1)
    def _():
        sweep(False)

    @pl.when(qi == nq - 1)
    def _():
        sweep(True)

    for p in range(SB_HEADS // 2):
        o_ref[:, p * LANES:(p + 1) * LANES] = jnp.where(low_half, acc_ref[2 * p], acc_ref[2 * p + 1])


def _sb_prompt(bias, q_hi, q_lo, kt_hi, kt_lo, v_hi, v_lo, *, batch, seq):
    n, d_sb = q_hi.shape
    nq = seq // TQ
    qspec = pl.BlockSpec((TQ, d_sb), lambda b, i: (b * nq + i, 0))
    kspec = pl.BlockSpec((1, d_sb, seq), lambda b, i: (b, 0, 0))
    vspec = pl.BlockSpec((seq, d_sb), lambda b, i: (b, 0))
    return pl.pallas_call(
        _sb_prompt_kernel,
        grid=(batch, nq),
        in_specs=[pl.BlockSpec(memory_space=pltpu.SMEM), qspec, qspec, kspec, kspec, vspec, vspec],
        out_specs=qspec,
        out_shape=jax.ShapeDtypeStruct((n, d_sb), F32),
        scratch_shapes=[pltpu.VMEM((SB_HEADS, TQ, LANES), F32), pltpu.VMEM((SB_HEADS, TQ, LANES), F32)],
        compiler_params=_cparams(2),
        name="sb_prompt",
    )(bias, q_hi, q_lo, kt_hi, kt_lo, v_hi, v_lo)


def _sb_sample_kernel(pt_ref, q_ref, bias_ref, *refs, pp):
    del pt_ref
    k_refs, v_refs = refs[:pp], refs[pp:2 * pp]
    o_ref, carry_ref, acc_ref = refs[2 * pp:]
    j = pl.program_id(1)
    d_sb = SB_HEADS * SB_HEAD_DIM
    page = k_refs[0].shape[-1]

    @pl.when(j == 0)
    def _():
        carry_ref[...] = jnp.zeros_like(carry_ref)
        acc_ref[...] = jnp.zeros_like(acc_ref)

    head_of_lane = lax.broadcasted_iota(I32, (SB_HEADS, d_sb), 1) // SB_HEAD_DIM
    row = lax.broadcasted_iota(I32, (SB_HEADS, d_sb), 0)
    own = head_of_lane == row
    q_rows = jnp.broadcast_to(q_ref[0], (SB_HEADS, d_sb))
    q_bd = jnp.where(own, q_rows, 0.0)
    r = lax.broadcasted_iota(I32, (page, page), 0)
    c = lax.broadcasted_iota(I32, (page, page), 1)
    tri = jnp.where(r >= c, 1.0, 0.0).astype(BF16)

    def stacked_split(a):
        c = a * SPLIT_FACTOR
        hi = c - (c - a)
        return jnp.concatenate([hi, a - hi], axis=0).astype(BF16)

    def stack3(a, b, nt=False):
        b_hi, b_lo = _split(b)
        dot = _dot_nt if nt else _dot
        m = a.shape[0]
        both = dot(stacked_split(a), b_hi)
        return both[:m] + (both[m:] + dot(a.astype(BF16), b_lo))

    order = list(reversed(range(pp)))
    zs = [stack3(q_bd, k_refs[pg][0].reshape(d_sb, page)) + bias_ref[...] for pg in order]
    sps = [_softplus(z) for z in zs]
    totals = [jnp.broadcast_to(jnp.sum(sp, axis=1, keepdims=True), carry_ref.shape) for sp in sps]
    cums = []
    for sp in sps:
        both = _dot(stacked_split(sp), tri)
        cums.append(both[:SB_HEADS] + both[SB_HEADS:])
    carry = carry_ref[...]
    acc = acc_ref[...]
    for n, pg in enumerate(order):
        w = jnp.exp(zs[n] + carry - cums[n])
        vt = v_refs[pg][0].reshape(d_sb, page)
        acc = acc + stack3(w, vt, nt=True)
        carry = carry - totals[n]
    carry_ref[...] = carry
    acc_ref[...] = acc

    @pl.when(j == pl.num_programs(1) - 1)
    def _():
        o_ref[0] = jnp.sum(jnp.where(own, acc, 0.0), axis=0, keepdims=True)


def _sb_sample(page_table, q3, bias_b, kt_pages, vt_pages):
    db, n_pages = page_table.shape
    d_sb = q3.shape[-1]
    page = kt_pages.shape[-1]
    pp = PAGES_PER_STEP
    assert n_pages % pp == 0
    nchunk = n_pages // pp

    def page_spec(pg):
        return pl.BlockSpec((1, SB_HEADS, SB_HEAD_DIM, page),
                            lambda b, j, pt: (pt[b, (nchunk - 1 - j) * pp + pg], 0, 0, 0))

    grid_spec = pltpu.PrefetchScalarGridSpec(
        num_scalar_prefetch=1,
        grid=(db, nchunk),
        in_specs=[pl.BlockSpec((1, 1, d_sb), lambda b, j, pt: (b, 0, 0)),
                  pl.BlockSpec((SB_HEADS, page), lambda b, j, pt: (0, 0))]
        + [page_spec(pg) for pg in range(pp)] * 2,
        out_specs=pl.BlockSpec((1, 1, d_sb), lambda b, j, pt: (b, 0, 0)),
        scratch_shapes=[pltpu.VMEM((SB_HEADS, page), F32), pltpu.VMEM((SB_HEADS, d_sb), F32)],
    )
    return pl.pallas_call(
        functools.partial(_sb_sample_kernel, pp=pp),
        grid_spec=grid_spec,
        out_shape=jax.ShapeDtypeStruct((db, 1, d_sb), F32),
        compiler_params=_cparams(2),
        name="sb_sample",
    )(page_table, q3, bias_b, *([kt_pages] * pp), *([vt_pages] * pp))


def _route(hn2, wr_hi_ref, wr_lo_ref, rb_ref, dense):
    m = hn2.shape[0]
    x_hi, x_lo = _split(hn2)
    w_hi, w_lo = wr_hi_ref[...], wr_lo_ref[...]
    lg = (_dot_nt(w_hi, x_hi) + _dot_nt(w_lo, x_hi)) + (_dot_nt(w_hi, x_lo) + _dot_nt(w_lo, x_lo)) + rb_ref[...]

    g = [lg[k:k + 1, :] for k in range(N_EXPERT_GROUPS)]
    gmax = functools.reduce(jnp.maximum, g)
    gidx = jnp.full((1, m), N_EXPERT_GROUPS - 1, I32)
    for k in reversed(range(N_EXPERT_GROUPS - 1)):
        gidx = jnp.where(g[k] == gmax, k, gidx)
    p_group = 1.0 / functools.reduce(lambda a, b: a + b, [jnp.exp(gk - gmax) for gk in g])

    e = []
    for k in range(EXPERTS_PER_GROUP):
        last = 8 + (N_EXPERT_GROUPS - 1) * EXPERTS_PER_GROUP + k
        ek = lg[last:last + 1, :]
        for gi in reversed(range(N_EXPERT_GROUPS - 1)):
            row = 8 + gi * EXPERTS_PER_GROUP + k
            ek = jnp.where(gidx == gi, lg[row:row + 1, :], ek)
        e.append(ek)
    v1 = functools.reduce(jnp.maximum, e)
    i1 = jnp.full((1, m), EXPERTS_PER_GROUP - 1, I32)
    for k in reversed(range(EXPERTS_PER_GROUP - 1)):
        i1 = jnp.where(e[k] == v1, k, i1)
    e2 = [jnp.where(i1 == k, -jnp.inf, e[k]) for k in range(EXPERTS_PER_GROUP)]
    v2 = functools.reduce(jnp.maximum, e2)
    i2 = jnp.full((1, m), EXPERTS_PER_GROUP - 1, I32)
    for k in reversed(range(EXPERTS_PER_GROUP - 1)):
        i2 = jnp.where(e2[k] == v2, k, i2)
    t = jnp.exp(v2 - v1)
    w1 = p_group / (1.0 + t)
    w2 = p_group * t / (1.0 + t)
    rows = lax.broadcasted_iota(I32, (LANES, m), 0)

    if dense:
        e1 = gidx * EXPERTS_PER_GROUP + i1
        e2id = gidx * EXPERTS_PER_GROUP + i2
        coef_t = jnp.where(rows == e1, w1, 0.0) + jnp.where(rows == e2id, w2, 0.0)
        return None, coef_t.T

    first_low = i1 < i2
    ia = jnp.where(first_low, i1, i2)
    ib = jnp.where(first_low, i2, i1)
    ca = jnp.where(first_low, w1, w2)
    cb = jnp.where(first_low, w2, w1)
    pidx = jnp.zeros((1, m), I32)
    for n, (pa, pb) in enumerate(PAIRS):
        pidx = jnp.where(jnp.logical_and(ia == pa, ib == pb), n, pidx)
    bucket = gidx * len(PAIRS) + pidx
    coef_t = jnp.where(rows == 0, ca, jnp.where(rows == 1, cb, 0.0))
    return bucket, coef_t.T


def _ffn_norm_route_store(h1, gf_ref, wr_hi_ref, wr_lo_ref, rb_ref, h1_ref, xe_ref, bk_ref):
    d = h1.shape[1]
    hn2 = _rms(h1, gf_ref[...])
    bucket, coef = _route(hn2, wr_hi_ref, wr_lo_ref, rb_ref, dense=bk_ref is None)
    h1_ref[...] = h1
    xe_ref[:, :d] = hn2
    xe_ref[:, d:] = coef
    if bk_ref is not None:
        bk_ref[...] = bucket


def _pool_windows(xp_ref, inv_cnt):
    m = xp_ref.shape[0] - HALO_POOL
    gd = xp_ref.shape[1] // len(POOL_WINDOWS)
    outs = []
    for gi, w in enumerate(POOL_WINDOWS):
        lanes = slice(gi * gd, (gi + 1) * gd)
        s = xp_ref[HALO_POOL:HALO_POOL + m, lanes]
        cur = s
        for j in range(1, w):
            s = s + xp_ref[HALO_POOL - j:HALO_POOL - j + m, lanes]
        outs.append(s * inv_cnt(w) - cur)
    return outs


def _mix_ab_kernel(hp_ref, osbp_ref, up_ref, uprev_ref, hs_ref, osbs_ref, us_ref, hist_ref,
                   wpool_ref, wpool_lo_ref, pscale_ref, wout_ref, wout_lo_ref, gf_ref, wr_hi_ref, wr_lo_ref, rb_ref,
                   h1p_ref, xep_ref, bkp_ref, h1s_ref, xes_ref, xp_ref, *, npt, tps, pos0):
    i = pl.program_id(0)
    d_sb = osbp_ref.shape[1]

    def tail(pooled, osb, h, precise_osb):
        o_pool = jnp.concatenate(
            [_dot3(*_split(pooled[gi]), wpool_ref[gi], wpool_lo_ref[gi]) for gi in range(len(POOL_WINDOWS))],
            axis=1) * pscale_ref[...]
        mix = (_mm(osb, wout_ref, wout_lo_ref, precise_osb, rows=slice(0, d_sb))
               + _mm(o_pool, wout_ref, wout_lo_ref, True, rows=slice(d_sb, None)))
        return h + mix

    tin = i % tps

    def prompt_tile(precise_osb):
        xp_ref[:HALO_POOL, :] = jnp.where(tin == 0, 0.0, uprev_ref[...])
        xp_ref[HALO_POOL:, :] = up_ref[...]
        pos = tin * TM + lax.broadcasted_iota(I32, (TM, 1), 0)

        def inv_cnt(w):
            return 1.0 / jnp.minimum(pos + 1, w).astype(F32)

        h1 = tail(_pool_windows(xp_ref, inv_cnt), osbp_ref[...], hp_ref[...], precise_osb)
        _ffn_norm_route_store(h1, gf_ref, wr_hi_ref, wr_lo_ref, rb_ref, h1p_ref, xep_ref, bkp_ref)

    @pl.when(jnp.logical_and(i < npt, tin == tps - 1))
    def _():
        prompt_tile(True)

    @pl.when(jnp.logical_and(i < npt, tin != tps - 1))
    def _():
        prompt_tile(False)

    @pl.when(i == npt)
    def _():
        gd = us_ref.shape[1] // len(POOL_WINDOWS)
        db = hist_ref.shape[1]
        u = us_ref[...]
        pooled = []
        for gi, w in enumerate(POOL_WINDOWS):
            lanes = slice(gi * gd, (gi + 1) * gd)
            s = u[:db, lanes]
            for j in range(1, w):
                s = s + hist_ref[POOL_HIST - j][:, lanes]
            s = jnp.concatenate([s, u[db:, lanes]], axis=0)
            pooled.append(s * (1.0 / min(pos0 + 1, w)) - u[:, lanes])
        h1 = tail(pooled, osbs_ref[...], hs_ref[...], True)
        _ffn_norm_route_store(h1, gf_ref, wr_hi_ref, wr_lo_ref, rb_ref, h1s_ref, xes_ref, None)


def _mix_ab(h_p, osb_p, u_p, h_s, osb_s, u_s, hist, wpool, wpool_lo, pscale, wout, wout_lo, gf, wr_hi, wr_lo, rb, *, seq, pos0):
    n, d = h_p.shape
    npt = n // TM
    tps = seq // TM
    d_sb = osb_p.shape[1]
    d_pool = u_p.shape[1]
    hb = TM // HALO_POOL
    consts = (wpool, wpool_lo, pscale, wout, wout_lo, gf, wr_hi, wr_lo, rb)
    in_specs = [
        _pspec(d, npt), _pspec(d_sb, npt), _pspec(d_pool, npt),
        pl.BlockSpec((HALO_POOL, d_pool), lambda i: (jnp.maximum(jnp.minimum(i, npt - 1) * hb - 1, 0), 0)),
        _const_spec((SP, d)), _const_spec((SP, d_sb)), _const_spec((SP, d_pool)), _const_spec(hist.shape),
    ] + [_const_spec(c.shape) for c in consts]
    de = d + LANES
    out_shape = (
        jax.ShapeDtypeStruct((n, d), F32), jax.ShapeDtypeStruct((n, de), F32), jax.ShapeDtypeStruct((1, n), I32),
        jax.ShapeDtypeStruct((SP, d), F32), jax.ShapeDtypeStruct((SP, de), F32),
    )
    out_specs = (
        _pspec(d, npt), _pspec(de, npt), pl.BlockSpec((1, TM), lambda i: (0, jnp.minimum(i, npt - 1))),
        _const_spec((SP, d)), _const_spec((SP, de)),
    )
    return pl.pallas_call(
        functools.partial(_mix_ab_kernel, npt=npt, tps=tps, pos0=pos0),
        grid=(npt + 1,),
        in_specs=in_specs, out_specs=out_specs, out_shape=out_shape,
        scratch_shapes=[pltpu.VMEM((HALO_POOL + TM, d_pool), F32)],
        compiler_params=_cparams(),
        name="mix_ab",
    )(h_p, osb_p, u_p, u_p, h_s, osb_s, u_s, hist, *consts)


def _sort_kernel(b_ref, pos_ref, tiles_ref, carry_ref, start_ref):
    ps = pl.program_id(0)
    j = pl.program_id(1)
    cw = b_ref.shape[1]
    rows = lax.broadcasted_iota(I32, (NB_ROWS, cw), 0)
    onehot = rows == b_ref[...]
    oh = jnp.where(onehot, 1.0, 0.0)
    counts = jnp.broadcast_to(jnp.sum(oh, axis=1, keepdims=True), (NB_ROWS, LANES))

    @pl.when(jnp.logical_and(ps == 0, j == 0))
    def _():
        carry_ref[...] = jnp.zeros_like(carry_ref)

    @pl.when(ps == 0)
    def _():
        carry_ref[...] += counts

    @pl.when(jnp.logical_and(ps == 1, j == 0))
    def _():
        real = lax.broadcasted_iota(I32, (NB_ROWS, LANES), 0) < N_BUCKETS
        tiles = jnp.where(real, jnp.floor((carry_ref[...] + (TMX - 1)) * (1.0 / TMX)), 0.0)
        rr = lax.broadcasted_iota(I32, (NB_ROWS, NB_ROWS), 0)
        cc = lax.broadcasted_iota(I32, (NB_ROWS, NB_ROWS), 1)
        below = jnp.where(cc < rr, 1.0, 0.0).astype(BF16)
        start_ref[...] = _dot(below, tiles.astype(BF16)) * TMX
        tiles_ref[...] = tiles
        carry_ref[...] = jnp.zeros_like(carry_ref)

    @pl.when(ps == 1)
    def _():
        rj = lax.broadcasted_iota(I32, (cw, cw), 0)
        ct = lax.broadcasted_iota(I32, (cw, cw), 1)
        before = jnp.where(rj < ct, 1.0, 0.0).astype(BF16)
        rank = _dot(oh.astype(BF16), before)
        val = rank + carry_ref[:, 0:1] + start_ref[:, 0:1]
        pos_ref[...] = jnp.sum(jnp.where(onehot, val, 0.0), axis=0, keepdims=True).astype(I32)
        carry_ref[...] += counts


def _sort_positions(buckets):
    npad = buckets.shape[1]
    nch = npad // SORT_CHUNK
    return pl.pallas_call(
        _sort_kernel,
        grid=(2, nch),
        in_specs=[pl.BlockSpec((1, SORT_CHUNK), lambda ps, j: (0, j))],
        out_specs=(pl.BlockSpec((1, SORT_CHUNK), lambda ps, j: (0, j * ps)),
                   pl.BlockSpec((NB_ROWS, LANES), lambda ps, j: (0, 0))),
        out_shape=(jax.ShapeDtypeStruct((1, npad), I32), jax.ShapeDtypeStruct((NB_ROWS, LANES), F32)),
        scratch_shapes=[pltpu.VMEM((NB_ROWS, LANES), F32), pltpu.VMEM((NB_ROWS, LANES), F32)],
        compiler_params=_cparams(2),
        name="moe_sort",
    )(buckets)


def _move_rows_kernel(pos_ref, *refs, scatter):
    if scatter:
        tok_ref, _, sorted_ref, sems = refs
    else:
        sorted_ref, tok_ref, sems = refs
    nb = tok_ref.shape[0] // MOVE_BATCH

    def copy(tok_rows, sorted_rows, slot):
        src, dst = (tok_ref.at[tok_rows], sorted_ref.at[sorted_rows])
        if not scatter:
            src, dst = dst, src
        return pltpu.make_async_copy(src, dst, sems.at[slot])

    def batch_wait(slot):
        copy(pl.ds(0, MOVE_BATCH), pl.ds(0, MOVE_BATCH), slot).wait()

    def batch(b, _):
        slot = lax.rem(b, MOVE_SEMS)

        @pl.when(b >= MOVE_SEMS)
        def _():
            batch_wait(slot)

        def row(r, _):
            t = b * MOVE_BATCH + r
            copy(pl.ds(t, 1), pl.ds(pos_ref[0, t], 1), slot).start()
            return 0

        lax.fori_loop(0, MOVE_BATCH, row, 0, unroll=8)
        return 0

    lax.fori_loop(0, nb, batch, 0)

    def drain(b, _):
        batch_wait(lax.rem(b, MOVE_SEMS))
        return 0

    lax.fori_loop(max(nb - MOVE_SEMS, 0), nb, drain, 0)


def _scatter_rows(pos, src, dst):
    n, cols = src.shape
    chunk = min(MOVE_CHUNK, n)
    assert n % chunk == 0 and chunk % MOVE_BATCH == 0
    return pl.pallas_call(
        functools.partial(_move_rows_kernel, scatter=True),
        grid=(n // chunk,),
        in_specs=[pl.BlockSpec((1, chunk), lambda i: (0, i), memory_space=pltpu.SMEM),
                  pl.BlockSpec((chunk, cols), lambda i: (i, 0)),
                  pl.BlockSpec(memory_space=pl.ANY)],
        out_specs=pl.BlockSpec(memory_space=pl.ANY),
        out_shape=jax.ShapeDtypeStruct(dst.shape, dst.dtype),
        scratch_shapes=[pltpu.SemaphoreType.DMA((MOVE_SEMS,))],
        input_output_aliases={2: 0},
        compiler_params=_cparams(),
        name="moe_scatter",
    )(pos, src, dst)


def _gather_rows(pos, src):
    n = pos.shape[1]
    cols = src.shape[1]
    chunk = min(MOVE_CHUNK, n)
    assert n % chunk == 0 and chunk % MOVE_BATCH == 0
    return pl.pallas_call(
        functools.partial(_move_rows_kernel, scatter=False),
        grid=(n // chunk,),
        in_specs=[pl.BlockSpec((1, chunk), lambda i: (0, i), memory_space=pltpu.SMEM),
                  pl.BlockSpec(memory_space=pl.ANY)],
        out_specs=pl.BlockSpec((chunk, cols), lambda i: (i, 0)),
        out_shape=jax.ShapeDtypeStruct((n, cols), src.dtype),
        scratch_shapes=[pltpu.SemaphoreType.DMA((MOVE_SEMS,))],
        compiler_params=_cparams(),
        name="moe_gather",
    )(pos, src)


def _experts_kernel(ea_ref, eb_ref, valid_ref, xs_ref, wga_ref, wua_ref, wda_ref, wgb_ref, wub_ref, wdb_ref, ys_ref):
    i = pl.program_id(0)
    d = ys_ref.shape[1]

    @pl.when(valid_ref[i] == 1)
    def _():
        x = xs_ref[:, :d].astype(BF16)

        def expert(wg_ref, wu_ref, wd_ref):
            g = _dot(x, wg_ref[0].astype(BF16))
            u = _dot(x, wu_ref[0].astype(BF16))
            hid = g * _sigmoid(g) * u
            return _dot(hid.astype(BF16), wd_ref[0].astype(BF16))

        ya = expert(wga_ref, wua_ref, wda_ref)
        yb = expert(wgb_ref, wub_ref, wdb_ref)
        ys_ref[...] = xs_ref[:, d:d + 1] * ya + xs_ref[:, d + 1:d + 2] * yb

    @pl.when(valid_ref[i] == 0)
    def _():
        ys_ref[...] = jnp.zeros_like(ys_ref)


def _experts(ea, eb, valid, xs, w_gate, w_up, w_down):
    ns, de = xs.shape
    d = de - LANES
    dx = w_gate.shape[-1]
    ntiles = ns // TMX

    def wspec(shape, which):
        return pl.BlockSpec((1,) + shape, lambda i, ea, eb, valid: ((ea, eb)[which][i], 0, 0))

    grid_spec = pltpu.PrefetchScalarGridSpec(
        num_scalar_prefetch=3,
        grid=(ntiles,),
        in_specs=[pl.BlockSpec((TMX, de), lambda i, ea, eb, valid: (i, 0)),
                  wspec((d, dx), 0), wspec((d, dx), 0), wspec((dx, d), 0),
                  wspec((d, dx), 1), wspec((d, dx), 1), wspec((dx, d), 1)],
        out_specs=pl.BlockSpec((TMX, d), lambda i, ea, eb, valid: (i, 0)),
    )
    return pl.pallas_call(
        _experts_kernel,
        grid_spec=grid_spec,
        out_shape=jax.ShapeDtypeStruct((ns, d), F32),
        compiler_params=_cparams(),
        name="moe_experts",
    )(ea, eb, valid, xs, w_gate, w_up, w_down, w_gate, w_up, w_down)


def _moe_sparse(xe, bk, w_gate, w_up, w_down, layer):
    n_tok, de = xe.shape
    npad = -(-n_tok // SORT_CHUNK) * SORT_CHUNK
    buckets = jnp.concatenate([bk, jnp.full((1, npad - n_tok), N_BUCKETS, I32)], axis=1)
    pos, tiles = _sort_positions(buckets)

    ntiles = -(-n_tok // TMX) + N_BUCKETS
    tiles_per_bucket = tiles[:N_BUCKETS, 0].astype(I32)
    cum = jnp.cumsum(tiles_per_bucket)
    tile_ids = jnp.arange(ntiles, dtype=I32)
    valid = tile_ids < cum[-1]
    tb = jnp.sum((jnp.minimum(tile_ids, cum[-1] - 1)[:, None] >= cum[None, :]).astype(I32), axis=1)
    group, pidx = tb // len(PAIRS), tb % len(PAIRS)
    pa = jnp.asarray([p[0] for p in PAIRS], I32)[pidx]
    pb = jnp.asarray([p[1] for p in PAIRS], I32)[pidx]
    first = layer * N_EXPERTS + group * EXPERTS_PER_GROUP
    flat = lambda w: w.reshape((-1,) + w.shape[2:])

    pos = pos[:, :n_tok]
    xs = _scatter_rows(pos, xe, jnp.zeros((ntiles * TMX, de), F32))
    ys = _experts(first + pa, first + pb, valid.astype(I32), xs, flat(w_gate), flat(w_up), flat(w_down))
    return _gather_rows(pos, ys)


def _experts_dense_kernel(xe_ref, wg_ref, wu_ref, wd_ref, y_ref):
    e = pl.program_id(0)
    d = y_ref.shape[1]

    @pl.when(e == 0)
    def _():
        y_ref[...] = jnp.zeros_like(y_ref)

    x_hi, x_lo = _split(xe_ref[:, :d])
    g = _dot3(x_hi, x_lo, *_split(wg_ref[0, 0]))
    u = _dot3(x_hi, x_lo, *_split(wu_ref[0, 0]))
    hid = g * _sigmoid(g) * u
    y = _dot3(*_split(hid), *_split(wd_ref[0, 0]))
    lane = lax.broadcasted_iota(I32, (xe_ref.shape[0], LANES), 1)
    coef = jnp.sum(jnp.where(lane == e, xe_ref[:, d:], 0.0), axis=1, keepdims=True)
    y_ref[...] += coef * y


def _moe_dense(xe, w_gate, w_up, w_down, layer):
    m, de = xe.shape
    d = de - LANES
    _, n_exp, _, dx = w_gate.shape
    return pl.pallas_call(
        _experts_dense_kernel,
        grid=(n_exp,),
        in_specs=[_const_spec((m, de)),
                  pl.BlockSpec((1, 1, d, dx), lambda e: (layer, e, 0, 0)), pl.BlockSpec((1, 1, d, dx), lambda e: (layer, e, 0, 0)),
                  pl.BlockSpec((1, 1, dx, d), lambda e: (layer, e, 0, 0))],
        out_specs=_const_spec((m, d)),
        out_shape=jax.ShapeDtypeStruct((m, d), F32),
        compiler_params=_cparams(),
        name="moe_dense",
    )(xe, w_gate, w_up, w_down)


def _ple(h, y, p, gpg_ref, wpg_ref, wpg_lo_ref, bpg_ref, wple_ref, wple_lo_ref, precise):
    h2 = h + y
    gate = _sigmoid(_mm(_rms(h2, gpg_ref[...]), wpg_ref, wpg_lo_ref, precise) + bpg_ref[...])
    return h2 + gate * _mm(p, wple_ref, wple_lo_ref, precise)


def _ple_inproj_c_kernel(hp_ref, yp_ref, pp_ref, hs_ref, ys_ref, ps_ref,
                         gpg_ref, wpg_ref, wpg_lo_ref, bpg_ref, wple_ref, wple_lo_ref, gm_ref, win_ref, win_lo_ref,
                         h3p_ref, ucp_ref, h3s_ref, ucs_ref, *, npt):
    i = pl.program_id(0)

    def run(h_ref, y_ref, p_ref, h3_ref, uc_ref, precise):
        h3 = _ple(h_ref[...], y_ref[...], p_ref[...], gpg_ref, wpg_ref, wpg_lo_ref, bpg_ref, wple_ref, wple_lo_ref, precise)
        h3_ref[...] = h3
        ag = _mm(_rms(h3, gm_ref[...]), win_ref, win_lo_ref, precise)
        dc = ag.shape[1] // 2
        uc_ref[...] = ag[:, :dc] * _sigmoid(ag[:, dc:])

    @pl.when(i < npt)
    def _():
        run(hp_ref, yp_ref, pp_ref, h3p_ref, ucp_ref, False)

    @pl.when(i == npt)
    def _():
        run(hs_ref, ys_ref, ps_ref, h3s_ref, ucs_ref, True)


def _ple_inproj_c(h_p, y_p, p_all, layer, h_s, y_s, p_s, gpg, wpg, wpg_lo, bpg, wple, wple_lo, gm, win, win_lo):
    n, d = h_p.shape
    npt = n // TM
    dple = p_all.shape[1]
    dc = win.shape[1] // 2
    consts = (gpg, wpg, wpg_lo, bpg, wple, wple_lo, gm, win, win_lo)
    return pl.pallas_call(
        functools.partial(_ple_inproj_c_kernel, npt=npt),
        grid=(npt + 1,),
        in_specs=[_pspec(d, npt), _pspec(d, npt), _pspec(dple, npt, first_block=layer * npt),
                  _const_spec((SP, d)), _const_spec((SP, d)), _const_spec((SP, dple))]
        + [_const_spec(c.shape) for c in consts],
        out_specs=(_pspec(d, npt), _pspec(dc, npt), _const_spec((SP, d)), _const_spec((SP, dc))),
        out_shape=(jax.ShapeDtypeStruct((n, d), F32), jax.ShapeDtypeStruct((n, dc), F32),
                   jax.ShapeDtypeStruct((SP, d), F32), jax.ShapeDtypeStruct((SP, dc), F32)),
        compiler_params=_cparams(),
        name="ple_inproj_c",
    )(h_p, y_p, p_all, h_s, y_s, p_s, *consts)


def _ple_final_kernel(hp_ref, yp_ref, pp_ref, hs_ref, ys_ref, ps_ref,
                      gpg_ref, wpg_ref, wpg_lo_ref, bpg_ref, wple_ref, wple_lo_ref, gfin_ref, op_ref, os_ref, *, npt):
    i = pl.program_id(0)

    def run(h_ref, y_ref, p_ref, o_ref, precise):
        h = _ple(h_ref[...], y_ref[...], p_ref[...], gpg_ref, wpg_ref, wpg_lo_ref, bpg_ref, wple_ref, wple_lo_ref, precise)
        o_ref[...] = _rms(h, gfin_ref[...])

    @pl.when(i < npt)
    def _():
        run(hp_ref, yp_ref, pp_ref, op_ref, False)

    @pl.when(i == npt)
    def _():
        run(hs_ref, ys_ref, ps_ref, os_ref, True)


def _ple_final(h_p, y_p, p_all, layer, h_s, y_s, p_s, gpg, wpg, wpg_lo, bpg, wple, wple_lo, gfin):
    n, d = h_p.shape
    npt = n // TM
    dple = p_all.shape[1]
    consts = (gpg, wpg, wpg_lo, bpg, wple, wple_lo, gfin)
    return pl.pallas_call(
        functools.partial(_ple_final_kernel, npt=npt),
        grid=(npt + 1,),
        in_specs=[_pspec(d, npt), _pspec(d, npt), _pspec(dple, npt, first_block=layer * npt),
                  _const_spec((SP, d)), _const_spec((SP, d)), _const_spec((SP, dple))]
        + [_const_spec(c.shape) for c in consts],
        out_specs=(_pspec(d, npt), _const_spec((SP, d))),
        out_shape=(jax.ShapeDtypeStruct((n, d), F32), jax.ShapeDtypeStruct((SP, d), F32)),
        compiler_params=_cparams(),
        name="ple_final",
    )(h_p, y_p, p_all, h_s, y_s, p_s, *consts)


def _conv_tail(y, h, bdw_ref, lng_ref, lnb_ref, wout_ref, wout_lo_ref, precise):
    y = y + bdw_ref[...]
    mu = jnp.mean(y, axis=-1, keepdims=True)
    yc = y - mu
    yn = yc * lax.rsqrt(jnp.mean(yc * yc, axis=-1, keepdims=True) + EPS) * lng_ref[...] + lnb_ref[...]
    act = yn * _sigmoid(yn)
    return h + _mm(act, wout_ref, wout_lo_ref, precise)


def _mix_c_kernel(hp_ref, ucp_ref, ucprev_ref, hs_ref, ucs_ref, hist_ref,
                  wdw_ref, bdw_ref, lng_ref, lnb_ref, wout_ref, wout_lo_ref, gf_ref, wr_hi_ref, wr_lo_ref, rb_ref,
                  h1p_ref, xep_ref, bkp_ref, h1s_ref, xes_ref, xp_ref, sh_ref, wb_ref, y_ref, *, npt, tps):
    i = pl.program_id(0)

    @pl.when(i < npt)
    def _():
        tin = i % tps
        xp_ref[:HALO_CONV, :] = jnp.where(tin == 0, 0.0, ucprev_ref[...])
        xp_ref[HALO_CONV:, :] = ucp_ref[...]
        off = HALO_CONV - CONV_HIST
        half = TM // 2
        ext = sh_ref.shape[1]
        groups = CONV_ROWS // CONV_SUBLANES
        for j in range(CONV_WIDTH):
            wb_ref[j] = jnp.broadcast_to(wdw_ref[j:j + 1, :], wb_ref.shape[1:])
        for hb in range(TM // half):
            base = hb * half
            for s in range(1, CONV_SUBLANES):
                sh_ref[s - 1] = xp_ref[base + s:base + s + ext, :]

            def chunk(c, _):
                r0 = pl.multiple_of(c * CONV_ROWS, CONV_ROWS)
                accs = [None] * groups
                for j in range(CONV_WIDTH):
                    s, k8 = (off + j) % CONV_SUBLANES, (off + j) // CONV_SUBLANES * CONV_SUBLANES
                    wb = wb_ref[j]
                    for g in range(groups):
                        rows = pl.ds(r0 + k8 + g * CONV_SUBLANES, CONV_SUBLANES)
                        x = xp_ref[pl.ds(base + r0 + k8 + g * CONV_SUBLANES, CONV_SUBLANES), :] if s == 0 else sh_ref[s - 1, rows, :]
                        accs[g] = wb * x if accs[g] is None else accs[g] + wb * x
                for g in range(groups):
                    y_ref[pl.ds(base + r0 + g * CONV_SUBLANES, CONV_SUBLANES), :] = accs[g]
                return 0

            lax.fori_loop(0, half // CONV_ROWS, chunk, 0)
        h1 = _conv_tail(y_ref[...], hp_ref[...], bdw_ref, lng_ref, lnb_ref, wout_ref, wout_lo_ref, False)
        _ffn_norm_route_store(h1, gf_ref, wr_hi_ref, wr_lo_ref, rb_ref, h1p_ref, xep_ref, bkp_ref)

    @pl.when(i == npt)
    def _():
        db = hist_ref.shape[1]
        y = wdw_ref[CONV_WIDTH - 1:CONV_WIDTH, :] * ucs_ref[...]
        y_hist = wdw_ref[0:1, :] * hist_ref[0]
        for j in range(1, CONV_HIST):
            y_hist = y_hist + wdw_ref[j:j + 1, :] * hist_ref[j]
        y = jnp.concatenate([y[:db] + y_hist, y[db:]], axis=0)
        h1 = _conv_tail(y, hs_ref[...], bdw_ref, lng_ref, lnb_ref, wout_ref, wout_lo_ref, True)
        _ffn_norm_route_store(h1, gf_ref, wr_hi_ref, wr_lo_ref, rb_ref, h1s_ref, xes_ref, None)


def _mix_c(h_p, uc_p, h_s, uc_s, hist, wdw, bdw, lng, lnb, wout, wout_lo, gf, wr_hi, wr_lo, rb, *, seq):
    n, d = h_p.shape
    npt = n // TM
    tps = seq // TM
    dc = uc_p.shape[1]
    hb = TM // HALO_CONV
    shifted_reach = (HALO_CONV - CONV_HIST + CONV_WIDTH - 2) // CONV_SUBLANES * CONV_SUBLANES
    assert (TM // 2) % CONV_ROWS == 0 and CONV_ROWS % CONV_SUBLANES == 0
    consts = (wdw, bdw, lng, lnb, wout, wout_lo, gf, wr_hi, wr_lo, rb)
    in_specs = [
        _pspec(d, npt), _pspec(dc, npt),
        pl.BlockSpec((HALO_CONV, dc), lambda i: (jnp.maximum(jnp.minimum(i, npt - 1) * hb - 1, 0), 0)),
        _const_spec((SP, d)), _const_spec((SP, dc)), _const_spec(hist.shape),
    ] + [_const_spec(c.shape) for c in consts]
    de = d + LANES
    out_shape = (
        jax.ShapeDtypeStruct((n, d), F32), jax.ShapeDtypeStruct((n, de), F32), jax.ShapeDtypeStruct((1, n), I32),
        jax.ShapeDtypeStruct((SP, d), F32), jax.ShapeDtypeStruct((SP, de), F32),
    )
    out_specs = (
        _pspec(d, npt), _pspec(de, npt), pl.BlockSpec((1, TM), lambda i: (0, jnp.minimum(i, npt - 1))),
        _const_spec((SP, d)), _const_spec((SP, de)),
    )
    return pl.pallas_call(
        functools.partial(_mix_c_kernel, npt=npt, tps=tps),
        grid=(npt + 1,),
        in_specs=in_specs, out_specs=out_specs, out_shape=out_shape,
        scratch_shapes=[pltpu.VMEM((HALO_CONV + TM, dc), F32),
                        pltpu.VMEM((CONV_SUBLANES - 1, TM // 2 + shifted_reach, dc), F32),
                        pltpu.VMEM((CONV_WIDTH, CONV_SUBLANES, dc), F32),
                        pltpu.VMEM((TM, dc), F32)],
        compiler_params=_cparams(),
        name="mix_c",
    )(h_p, uc_p, uc_p, h_s, uc_s, hist, *consts)


def _router_weights(w_rg, b_rg, w_re, b_re):
    d = w_rg.shape[0]
    wt = jnp.zeros((ROUTE_ROWS, d), F32).at[:N_EXPERT_GROUPS].set(w_rg.T).at[8:8 + w_re.shape[1]].set(w_re.T)
    hi, lo = _hi_lo(wt)
    rb = jnp.zeros((ROUTE_ROWS, 1), F32).at[:N_EXPERT_GROUPS, 0].set(b_rg).at[8:8 + w_re.shape[1], 0].set(b_re)
    return hi, lo, rb


def _pad_rows(x, rows):
    return jnp.concatenate([x, jnp.zeros((rows - x.shape[0],) + x.shape[1:], x.dtype)], axis=0)


def kernel(x_prompt, x_sample, p_prompt, p_sample, cache_k, cache_v, state_pool, state_conv, page_table, norm_mix, norm_ffn, norm_final, w_in_ab, sb_bias, w_pool, pool_scale, w_out_ab, w_in_c, w_dw, b_dw, ln_g_c, ln_b_c, w_out_c, w_rg, b_rg, w_re, b_re, w_gate_e, w_up_e, w_down_e, w_ple, w_ple_gate, b_ple_gate, g_ple_gate):
    batch, seq, d = x_prompt.shape
    db, dec_seq, _ = x_sample.shape
    depth = norm_mix.shape[0]
    d_sb = SB_HEADS * SB_HEAD_DIM
    n_pages = page_table.shape[1]
    page = cache_k.shape[2]
    pos0 = n_pages * page
    assert dec_seq == 1 and depth == 2 and db <= SP and db % 8 == 0 and w_gate_e.shape[1] == N_EXPERTS
    assert seq % TM == 0 and seq % TQ == 0 and TMX & (TMX - 1) == 0
    assert max(CONV_HIST, POOL_HIST) <= TQ <= TM
    n_p = batch * seq

    row = lambda v: v.reshape(1, -1)
    x_p = x_prompt.reshape(n_p, d)
    x_s = _pad_rows(x_sample.reshape(db, d), SP)
    p_all = p_prompt.reshape(depth * n_p, -1)
    ps = [_pad_rows(p_sample[i].reshape(db, -1), SP) for i in range(depth)]

    w_in, w_in_lo = _hi_lo(w_in_ab[0])
    w_kv_t, w_kv_t_lo = _hi_lo(w_in_ab[0][:, d_sb:3 * d_sb].T)
    qh_p, ql_p, vh_p, vl_p, u_p, kt_p, vt_p, kth_p, ktl_p, q_s, u_s, k_s, v_s = _inproj_ab(
        x_p, x_s, row(norm_mix[0]), w_in, w_in_lo, w_kv_t, w_kv_t_lo[:d_sb], batch=batch, seq=seq)
    osb_p = _sb_prompt(sb_bias[0], qh_p, ql_p, kth_p, ktl_p, vh_p, vl_p, batch=batch, seq=seq)
    kt_pages = jnp.transpose(cache_k[0], (0, 2, 3, 1))
    vt_pages = jnp.transpose(cache_v[0], (0, 2, 3, 1))
    bias_b = jnp.broadcast_to(sb_bias[0][:, None], (SB_HEADS, page))
    osb_s = _sb_sample(page_table, q_s[:db].reshape(db, 1, d_sb), bias_b, kt_pages, vt_pages)
    osb_s = _pad_rows(osb_s.reshape(db, d_sb), SP)

    pool_hist = jnp.transpose(state_pool[0], (1, 0, 2))
    wr_hi, wr_lo, rb = _router_weights(w_rg[0], b_rg[0], w_re[0], b_re[0])
    w_pool_hi, w_pool_lo = _hi_lo(w_pool[0])
    w_out_hi, w_out_lo = _hi_lo(w_out_ab[0])
    h1_p, xe_p, bk_p, h1_s, xe_s = _mix_ab(
        x_p, osb_p, u_p, x_s, osb_s, u_s, pool_hist, w_pool_hi, w_pool_lo, row(pool_scale[0]),
        w_out_hi, w_out_lo, row(norm_ffn[0]), wr_hi, wr_lo, rb, seq=seq, pos0=pos0)
    y_p = _moe_sparse(xe_p, bk_p, w_gate_e, w_up_e, w_down_e, 0)
    y_s = _moe_dense(xe_s, w_gate_e, w_up_e, w_down_e, 0)

    w_pg, w_pg_lo = _hi_lo(w_ple_gate[0])
    w_pl, w_pl_lo = _hi_lo(w_ple[0])
    w_ic, w_ic_lo = _hi_lo(w_in_c[0])
    h3_p, uc_p, h3_s, uc_s = _ple_inproj_c(
        h1_p, y_p, p_all, 0, h1_s, y_s, ps[0], row(g_ple_gate[0]), w_pg, w_pg_lo, row(b_ple_gate[0]),
        w_pl, w_pl_lo, row(norm_mix[1]), w_ic, w_ic_lo)
    conv_hist = jnp.transpose(state_conv[0], (1, 0, 2))
    wr_hi, wr_lo, rb = _router_weights(w_rg[1], b_rg[1], w_re[1], b_re[1])
    w_oc, w_oc_lo = _hi_lo(w_out_c[0])
    h4_p, xe_p, bk_p, h4_s, xe_s = _mix_c(
        h3_p, uc_p, h3_s, uc_s, conv_hist, w_dw[0], row(b_dw[0]), row(ln_g_c[0]), row(ln_b_c[0]),
        w_oc, w_oc_lo, row(norm_ffn[1]), wr_hi, wr_lo, rb, seq=seq)
    y_p = _moe_sparse(xe_p, bk_p, w_gate_e, w_up_e, w_down_e, 1)
    y_s = _moe_dense(xe_s, w_gate_e, w_up_e, w_down_e, 1)
    w_pg, w_pg_lo = _hi_lo(w_ple_gate[1])
    w_pl, w_pl_lo = _hi_lo(w_ple[1])
    out_p, out_s = _ple_final(
        h4_p, y_p, p_all, 1, h4_s, y_s, ps[1], row(g_ple_gate[1]), w_pg, w_pg_lo, row(b_ple_gate[1]),
        w_pl, w_pl_lo, row(norm_final))

    def heads_last(t):
        return jnp.transpose(t.reshape(batch, SB_HEADS, SB_HEAD_DIM, seq), (0, 3, 1, 2))[None]

    y_prompt = out_p.reshape(batch, seq, d)
    y_sample = out_s[:db].reshape(db, 1, d)
    k_prompt, v_prompt = heads_last(kt_p), heads_last(vt_p)
    pool_prompt = u_p.reshape(batch, seq, -1)[:, seq - POOL_HIST:][None]
    conv_prompt = uc_p.reshape(batch, seq, -1)[:, seq - CONV_HIST:][None]
    k_sample = k_s[:db].reshape(1, db, 1, SB_HEADS, SB_HEAD_DIM)
    v_sample = v_s[:db].reshape(1, db, 1, SB_HEADS, SB_HEAD_DIM)
    pool_sample = jnp.transpose(jnp.concatenate([pool_hist[1:], u_s[None, :db]], axis=0), (1, 0, 2))[None]
    conv_sample = jnp.transpose(jnp.concatenate([conv_hist[1:], uc_s[None, :db]], axis=0), (1, 0, 2))[None]
    return (y_prompt, y_sample, k_prompt, v_prompt, pool_prompt, conv_prompt, k_sample, v_sample, pool_sample, conv_sample)
```
